```python
import numpy as np
import jax, jax.numpy as jnp
from jax import lax

D_MODEL = 1024
BATCH = 8
SEQ = 2048
DEPTH = 4

CHUNK = 64
Q_BLOCK = 128
EPS = 1e-6
NEG_INF = -1e30

A_HEADS = 8
A_HEAD_DIM = 64
B_HEADS = 8
B_NOPE = 64
B_ROPE = 32
B_V = 64
B_Q_LORA = 384
B_KV_LORA = 256
ROPE_THETA = 10000.0
C_HEADS = 8
C_HEAD_DIM = 64
C_LEFT_CHUNKS = 8
C_BAND = (C_LEFT_CHUNKS + 1) * CHUNK
REL_CLIP = 256
N_BRANCH = 3
BRANCH_WIDTH = 512
D_FF = 4 * D_MODEL

A_W = A_HEADS * A_HEAD_DIM
C_W = C_HEADS * C_HEAD_DIM
B_QUP = B_HEADS * (B_NOPE + B_ROPE)
B_KVUP = B_HEADS * (B_NOPE + B_V)
IN_SIZES = (A_W, A_W, A_W, A_HEADS, B_Q_LORA, B_KV_LORA, B_ROPE, C_W, C_W, C_W, N_BRANCH * D_MODEL)
IN_WIDTH = sum(IN_SIZES)

kernel_name = "hybrid_fox_mla_chunkrel_gated_encoder"


def rms_norm(x, g):
    xf = x.astype(jnp.float32)
    y = xf * lax.rsqrt(jnp.mean(xf * xf, axis=-1, keepdims=True) + EPS)
    return (y * g.astype(jnp.float32)).astype(x.dtype)


def rope(x, pos):
    half = B_ROPE // 2
    inv = ROPE_THETA ** (-jnp.arange(half, dtype=jnp.float32) / half)
    ang = pos.astype(jnp.float32)[:, None] * inv[None, :]
    cos = jnp.cos(ang)[:, None, :]
    sin = jnp.sin(ang)[:, None, :]
    xf = x.astype(jnp.float32)
    x1, x2 = xf[..., :half], xf[..., half:]
    return jnp.concatenate([x1 * cos - x2 * sin, x2 * cos + x1 * sin], axis=-1).astype(x.dtype)


def prefix_block_attention(q, k, v, frame_causal, cum_logf=None):
    S = q.shape[1]
    scale = q.shape[-1] ** -0.5
    outs = []
    for i in range(S // Q_BLOCK):
        q0, q1 = i * Q_BLOCK, (i + 1) * Q_BLOCK
        s = jnp.einsum('bqhd,bkhd->bhqk', q[:, q0:q1], k[:, :q1],
                       preferred_element_type=jnp.float32) * scale
        qpos = jnp.arange(q0, q1)[:, None]
        kpos = jnp.arange(q1)[None, :]
        if frame_causal:
            mask = kpos <= qpos
        else:
            mask = (kpos // CHUNK) <= (qpos // CHUNK)
        if cum_logf is not None:
            s = s + (cum_logf[:, :, q0:q1, None] - cum_logf[:, :, None, :q1])
        p = jax.nn.softmax(jnp.where(mask, s, NEG_INF), axis=-1)
        outs.append(jnp.einsum('bhqk,bkhd->bqhd', p.astype(v.dtype), v[:, :q1]))
    return jnp.concatenate(outs, axis=1)


def chunk_band_attention(q, k, v, rel_bias):
    B, S, H, D = q.shape
    NC = S // CHUNK
    qc = q.reshape(B, NC, CHUNK, H, D)

    def band(t):
        tc = t.reshape(B, NC, CHUNK, H, D)
        tp = jnp.pad(tc, ((0, 0), (C_LEFT_CHUNKS, 0), (0, 0), (0, 0), (0, 0)))
        return jnp.concatenate([tp[:, j:j + NC] for j in range(C_LEFT_CHUNKS + 1)], axis=2)

    kb, vb = band(k), band(v)
    s = jnp.einsum('bnqhd,bnkhd->bhnqk', qc, kb, preferred_element_type=jnp.float32) * (D ** -0.5)
    qpos = jnp.arange(CHUNK)[:, None]
    kpos = jnp.arange(C_BAND)[None, :] - C_LEFT_CHUNKS * CHUNK
    rel = jnp.clip(qpos - kpos, -REL_CLIP, REL_CLIP) + REL_CLIP
    bias = rel_bias.astype(jnp.float32)[:, rel]
    valid = (jnp.arange(NC)[:, None] - C_LEFT_CHUNKS + jnp.arange(C_BAND)[None, :] // CHUNK) >= 0
    s = jnp.where(valid[None, None, :, None, :], s + bias[None, :, None], NEG_INF)
    p = jax.nn.softmax(s, axis=-1)
    o = jnp.einsum('bhnqk,bnkhd->bnqhd', p.astype(v.dtype), vb)
    return o.reshape(B, S, H, D)


def setup_inputs(seed: int = 0) -> dict:
    key = jax.random.key(seed)
    ks = jax.random.split(key, 20)
    f32 = jnp.float32

    def nrm(k, shape, fan_in, gain=1.0):
        return jax.random.normal(k, shape, f32) * (gain * fan_in ** -0.5)

    def gain(k, shape):
        return 1.0 + 0.02 * jax.random.normal(k, shape, f32)

    res = (2 * DEPTH) ** -0.5
    return {
        "x": jax.random.normal(ks[0], (BATCH, SEQ, D_MODEL), f32),
        "norm_mix": gain(ks[1], (DEPTH, D_MODEL)),
        "w_in": nrm(ks[2], (DEPTH, D_MODEL, IN_WIDTH), D_MODEL),
        "b_forget": jax.random.uniform(ks[3], (DEPTH, A_HEADS), f32, 1.0, 4.0),
        "b_gate": 0.1 * jax.random.normal(ks[4], (DEPTH, N_BRANCH * D_MODEL), f32),
        "qk_norm_a": gain(ks[5], (DEPTH, 2, A_HEAD_DIM)),
        "mla_q_norm": gain(ks[6], (DEPTH, B_Q_LORA)),
        "mla_kv_norm": gain(ks[7], (DEPTH, B_KV_LORA)),
        "w_q_up": nrm(ks[8], (DEPTH, B_Q_LORA, B_QUP), B_Q_LORA),
        "w_kv_up": nrm(ks[9], (DEPTH, B_KV_LORA, B_KVUP), B_KV_LORA),
        "qk_norm_b_nope": gain(ks[10], (DEPTH, 2, B_NOPE)),
        "qk_norm_b_rope": gain(ks[11], (DEPTH, 2, B_ROPE)),
        "qk_norm_c": gain(ks[12], (DEPTH, 2, C_HEAD_DIM)),
        "rel_bias": 0.1 * jax.random.normal(ks[13], (DEPTH, C_HEADS, 2 * REL_CLIP + 1), f32),
        "w_branch": nrm(ks[14], (DEPTH, N_BRANCH, BRANCH_WIDTH, D_MODEL), BRANCH_WIDTH),
        "w_out": nrm(ks[15], (DEPTH, D_MODEL, D_MODEL), D_MODEL, res),
        "norm_ffn": gain(ks[16], (DEPTH, D_MODEL)),
        "w_ff1": nrm(ks[17], (DEPTH, D_MODEL, D_FF), D_MODEL),
        "w_ff2": nrm(ks[18], (DEPTH, D_FF, D_MODEL), D_FF, res),
    }


def reference(x, norm_mix, w_in, b_forget, b_gate, qk_norm_a, mla_q_norm, mla_kv_norm,
              w_q_up, w_kv_up, qk_norm_b_nope, qk_norm_b_rope, qk_norm_c, rel_bias,
              w_branch, w_out, norm_ffn, w_ff1, w_ff2):
    B, S, _ = x.shape
    pos = jnp.arange(S)
    split_at = tuple(int(i) for i in np.cumsum(IN_SIZES)[:-1])
    for l in range(DEPTH):
        h = rms_norm(x, norm_mix[l])
        u = h @ w_in[l]
        (qa, ka, va, fa, qd, kvd, kr, qc, kc, vc, gl) = jnp.split(u, split_at, axis=-1)

        qa = rms_norm(qa.reshape(B, S, A_HEADS, A_HEAD_DIM), qk_norm_a[l, 0])
        ka = rms_norm(ka.reshape(B, S, A_HEADS, A_HEAD_DIM), qk_norm_a[l, 1])
        va = va.reshape(B, S, A_HEADS, A_HEAD_DIM)
        log_f = jax.nn.log_sigmoid(fa.astype(jnp.float32) + b_forget[l].astype(jnp.float32))
        cum_logf = jnp.cumsum(log_f, axis=1).transpose(0, 2, 1)
        ya = prefix_block_attention(qa, ka, va, True, cum_logf).reshape(B, S, BRANCH_WIDTH)

        qb = (rms_norm(qd, mla_q_norm[l]) @ w_q_up[l]).reshape(B, S, B_HEADS, B_NOPE + B_ROPE)
        kvb = (rms_norm(kvd, mla_kv_norm[l]) @ w_kv_up[l]).reshape(B, S, B_HEADS, B_NOPE + B_V)
        q_nope = rms_norm(qb[..., :B_NOPE], qk_norm_b_nope[l, 0])
        q_rope = rope(rms_norm(qb[..., B_NOPE:], qk_norm_b_rope[l, 0]), pos)
        k_nope = rms_norm(kvb[..., :B_NOPE], qk_norm_b_nope[l, 1])
        vb = kvb[..., B_NOPE:]
        k_rope = rope(rms_norm(kr.reshape(B, S, 1, B_ROPE), qk_norm_b_rope[l, 1]), pos)
        k_rope = jnp.broadcast_to(k_rope, (B, S, B_HEADS, B_ROPE))
        q_mla = jnp.concatenate([q_nope, q_rope], axis=-1)
        k_mla = jnp.concatenate([k_nope, k_rope], axis=-1)
        yb = prefix_block_attention(q_mla, k_mla, vb, False).reshape(B, S, BRANCH_WIDTH)

        qc = rms_norm(qc.reshape(B, S, C_HEADS, C_HEAD_DIM), qk_norm_c[l, 0])
        kc = rms_norm(kc.reshape(B, S, C_HEADS, C_HEAD_DIM), qk_norm_c[l, 1])
        vc = vc.reshape(B, S, C_HEADS, C_HEAD_DIM)
        yc = chunk_band_attention(qc, kc, vc, rel_bias[l]).reshape(B, S, BRANCH_WIDTH)

        y = jnp.stack([ya, yb, yc], axis=2)
        proj = jnp.einsum('bsnc,ncd->bsnd', y, w_branch[l])
        gates = jax.nn.sigmoid(gl + b_gate[l]).reshape(B, S, N_BRANCH, D_MODEL)
        merged = jnp.sum(gates * proj, axis=2)
        x = x + merged @ w_out[l]

        h = rms_norm(x, norm_ffn[l])
        x = x + jnp.square(jax.nn.relu(h @ w_ff1[l])) @ w_ff2[l]
    return x
```

```python
import functools

import numpy as np
import jax
import jax.numpy as jnp
from jax import lax
from jax.experimental import pallas as pl
from jax.experimental.pallas import tpu as pltpu

F32 = jnp.float32
BF16 = jnp.bfloat16

D_MODEL = 1024
DEPTH = 4
CHUNK = 64
EPS = 1e-6
NEG_INF = -1e30
HEADS = 8
HEAD_DIM = 64
B_NOPE = 64
B_ROPE = 32
B_Q_LORA = 384
B_KV_LORA = 256
ROPE_THETA = 10000.0
C_LEFT_CHUNKS = 8
REL_CLIP = 256
N_BRANCH = 3
BRANCH_WIDTH = 512
D_FF = 4 * D_MODEL
IN_SIZES = (512, 512, 512, HEADS, B_Q_LORA, B_KV_LORA, B_ROPE, 512, 512, 512, N_BRANCH * D_MODEL)
IN_OFFS = tuple(int(v) for v in np.cumsum((0,) + IN_SIZES))

LANES = 128
MXU_DIM = 256
SLOT = LANES
TOKEN_TILE = 512
ATTN_TQ = 256
VMEM_LIMIT = 56 * 1024 * 1024

PACK_QD = 6 * 512
PACK_KVD = PACK_QD + B_Q_LORA
PACK_MISC = PACK_KVD + B_KV_LORA
PACK_WIDTH = PACK_MISC + LANES
MISC_KR = 64


def _dot(a, b):
    return jnp.dot(a, b, preferred_element_type=F32)


def _dot_nt(a, b):
    return lax.dot_general(a, b, (((1,), (1,)), ((), ())), preferred_element_type=F32)


def _split2(v):
    hi = v.astype(BF16)
    lo = (v - hi.astype(F32)).astype(BF16)
    return hi, lo


def _group_sumsq(u, g_ref):
    hi, lo = _split2(u * u)
    g = g_ref[...]
    return _dot(hi, g) + _dot(lo, g)


def _row_rms(x, gain):
    ms = jnp.mean(x * x, axis=-1, keepdims=True)
    return x * lax.rsqrt(ms + EPS) * gain


def _const_spec(shape):
    nd = len(shape)
    return pl.BlockSpec(shape, lambda *_: (0,) * nd)


def _params(n_axes):
    return pltpu.CompilerParams(dimension_semantics=("parallel",) * n_axes,
                                vmem_limit_bytes=VMEM_LIMIT)


def _in_proj_kernel(x_ref, gmix_ref, w_ref, g64_ref, hgain_ref, gq_ref, gkv_ref,
                    qa_ref, ka_ref, va_ref, qc_ref, kc_ref, vc_ref, qd_ref, kvd_ref, misc_ref):
    h = _row_rms(x_ref[...], gmix_ref[...]).astype(BF16)

    def head_normed(col, gain_row):
        u = _dot(h, w_ref[:, col:col + 512])
        halves = []
        for j in range(2):
            ub = u[:, MXU_DIM * j:MXU_DIM * (j + 1)]
            ss = _group_sumsq(ub, g64_ref)
            halves.append(ub * lax.rsqrt(ss * (1.0 / HEAD_DIM) + EPS))
        return (jnp.concatenate(halves, axis=1) * hgain_ref[gain_row:gain_row + 1, :]).astype(BF16)

    qa_ref[...] = head_normed(0, 0)
    ka_ref[...] = head_normed(512, 1)
    va_ref[...] = _dot(h, w_ref[:, 1024:1536]).astype(BF16)
    qc_ref[...] = head_normed(1536, 2)
    kc_ref[...] = head_normed(2048, 3)
    vc_ref[...] = _dot(h, w_ref[:, 2560:3072]).astype(BF16)
    qd_ref[...] = _row_rms(_dot(h, w_ref[:, PACK_QD:PACK_KVD]), gq_ref[...]).astype(BF16)
    kvd_ref[...] = _row_rms(_dot(h, w_ref[:, PACK_KVD:PACK_MISC]), gkv_ref[...]).astype(BF16)
    misc_ref[...] = _dot(h, w_ref[:, PACK_MISC:PACK_WIDTH])


def _in_proj(x, gmix, w_pack, g64, hgain, gq, gkv):
    t = x.shape[0]
    tm = TOKEN_TILE
    row = lambda width: pl.BlockSpec((tm, width), lambda i: (i, 0))
    out_shapes = [jax.ShapeDtypeStruct((t, 512), BF16)] * 6 + [
        jax.ShapeDtypeStruct((t, B_Q_LORA), BF16),
        jax.ShapeDtypeStruct((t, B_KV_LORA), BF16),
        jax.ShapeDtypeStruct((t, LANES), F32)]
    return pl.pallas_call(
        _in_proj_kernel,
        grid=(t // tm,),
        in_specs=[row(D_MODEL), _const_spec(gmix.shape), _const_spec(w_pack.shape),
                  _const_spec(g64.shape), _const_spec(hgain.shape), _const_spec(gq.shape),
                  _const_spec(gkv.shape)],
        out_specs=[row(512)] * 6 + [row(B_Q_LORA), row(B_KV_LORA), row(LANES)],
        out_shape=out_shapes,
        compiler_params=_params(1),
        name="in_proj",
    )(x, gmix, w_pack, g64, hgain, gq, gkv)


def _fox_prep_kernel(misc_ref, bf_ref, tri_ref, c_ref, ct_ref):
    blk = MXU_DIM
    seq = misc_ref.shape[0]
    lane = lax.broadcasted_iota(jnp.int32, (blk, LANES), 1)
    tri = tri_ref[...]
    carry = jnp.zeros((1, LANES), F32)
    for j in range(seq // blk):
        z = misc_ref[j * blk:(j + 1) * blk, :] + bf_ref[...]
        lf = -(jnp.maximum(-z, 0.0) + jnp.log1p(jnp.exp(-jnp.abs(z))))
        lf = jnp.where(lane < HEADS, lf, 0.0)
        hi = lf.astype(BF16)
        r1 = lf - hi.astype(F32)
        mid = r1.astype(BF16)
        lo = (r1 - mid.astype(F32)).astype(BF16)
        c = _dot(tri, hi) + _dot(tri, mid) + _dot(tri, lo) + carry
        c_ref[0, j * blk:(j + 1) * blk, :] = c
        ct_ref[0, :, j * blk:(j + 1) * blk] = c.T[:HEADS, :]
        carry = c[blk - 1:blk, :]


def _fox_prep(misc, bf_row, tri, batch, seq):
    return pl.pallas_call(
        _fox_prep_kernel,
        grid=(batch,),
        in_specs=[pl.BlockSpec((seq, LANES), lambda b: (b, 0)),
                  _const_spec(bf_row.shape), _const_spec(tri.shape)],
        out_specs=[pl.BlockSpec((1, seq, LANES), lambda b: (b, 0, 0)),
                   pl.BlockSpec((1, HEADS, seq), lambda b: (b, 0, 0))],
        out_shape=[jax.ShapeDtypeStruct((batch, seq, LANES), F32),
                   jax.ShapeDtypeStruct((batch, HEADS, seq), F32)],
        compiler_params=_params(1),
        name="fox_prep",
    )(misc, bf_row, tri)


def _rope_slot(x, cos, sin, lo_half):
    partner = jnp.where(lo_half, pltpu.roll(x, LANES - B_ROPE // 2, 1), pltpu.roll(x, B_ROPE // 2, 1))
    return x * cos + partner * sin


def _mla_prep_kernel(qd_ref, kvd_ref, misc_ref, cos_ref, sin_ref, wq_ref, wk_ref, wv_ref,
                     gslot_ref, invcnt_ref, gq_ref, gk_ref, gkr_ref, q_ref, k_ref, v_ref):
    tm = qd_ref.shape[0]
    lane = lax.broadcasted_iota(jnp.int32, (tm, SLOT), 1)
    lo_half = lane < MISC_KR + B_ROPE // 2
    in_kr = (lane >= MISC_KR) & (lane < MISC_KR + B_ROPE)
    cos = cos_ref[...]
    sin = sin_ref[...]
    qd = qd_ref[...]
    kvd = kvd_ref[...]

    def slot_normed(u, gain_ref, col):
        ss = _group_sumsq(u, gslot_ref)
        return u * lax.rsqrt(ss * invcnt_ref[...] + EPS) * gain_ref[:, col:col + MXU_DIM]

    m = misc_ref[...]
    ss_kr = jnp.sum(jnp.where(in_kr, m * m, 0.0), axis=-1, keepdims=True)
    kr = _rope_slot(m * lax.rsqrt(ss_kr * (1.0 / B_ROPE) + EPS) * gkr_ref[...], cos, sin, lo_half)

    for j in range(HEADS * SLOT // MXU_DIM):
        col = j * MXU_DIM
        qn = slot_normed(_dot(qd, wq_ref[:, col:col + MXU_DIM]), gq_ref, col)
        kn = slot_normed(_dot(kvd, wk_ref[:, col:col + MXU_DIM]), gk_ref, col)
        for s in range(MXU_DIM // SLOT):
            sl = slice(s * SLOT, (s + 1) * SLOT)
            q_ref[:, col + s * SLOT:col + (s + 1) * SLOT] = _rope_slot(qn[:, sl], cos, sin, lo_half).astype(BF16)
            k_ref[:, col + s * SLOT:col + (s + 1) * SLOT] = (kn[:, sl] + kr).astype(BF16)
    v_ref[...] = _dot(kvd, wv_ref[...]).astype(BF16)


def _mla_prep(qdn, kvdn, misc, cos_t, sin_t, wq, wk, wv, gslot, invcnt, gq, gk, gkr, seq):
    t = qdn.shape[0]
    tm = TOKEN_TILE
    per_seq = seq // tm
    row = lambda width: pl.BlockSpec((tm, width), lambda i: (i, 0))
    pos = pl.BlockSpec((tm, SLOT), lambda i: (i % per_seq, 0))
    consts = [wq, wk, wv, gslot, invcnt, gq, gk, gkr]
    return pl.pallas_call(
        _mla_prep_kernel,
        grid=(t // tm,),
        in_specs=[row(B_Q_LORA), row(B_KV_LORA), row(LANES), pos, pos] + [_const_spec(c.shape) for c in consts],
        out_specs=[row(HEADS * SLOT), row(HEADS * SLOT), row(BRANCH_WIDTH)],
        out_shape=[jax.ShapeDtypeStruct((t, HEADS * SLOT), BF16),
                   jax.ShapeDtypeStruct((t, HEADS * SLOT), BF16),
                   jax.ShapeDtypeStruct((t, BRANCH_WIDTH), BF16)],
        compiler_params=_params(1),
        name="mla_prep",
    )(qdn, kvdn, misc, cos_t, sin_t, *consts)


def _softmax_pv(score_blocks, v_blocks):
    m = functools.reduce(jnp.maximum, [jnp.max(s, axis=-1, keepdims=True) for s in score_blocks])
    ps = [jnp.exp(s - m) for s in score_blocks]
    l = functools.reduce(jnp.add, [jnp.sum(p, axis=-1, keepdims=True) for p in ps])
    o = functools.reduce(jnp.add, [_dot(p.astype(BF16), v) for p, v in zip(ps, v_blocks)])
    return o / l


def _prefix_attn_kernel(*refs, frame_causal, decay, pair_masked_q):
    if decay:
        q_ref, k_ref, v_ref, c_ref, ct_ref, o_ref = refs
    else:
        q_ref, k_ref, v_ref, o_ref = refs
    seq = q_ref.shape[1]
    tq = ATTN_TQ
    pair = pl.program_id(1)
    lane = lax.broadcasted_iota(jnp.int32, (1, LANES), 1)
    row = lax.broadcasted_iota(jnp.int32, (tq, tq), 0)
    col = lax.broadcasted_iota(jnp.int32, (tq, tq), 1)
    if frame_causal:
        diag_ok = col <= row
    else:
        diag_ok = (col // CHUNK) <= (row // CHUNK)
    for i in range(seq // tq):
        q0, q1 = i * tq, (i + 1) * tq
        heads_out = []
        for hh in range(2):
            if pair_masked_q:
                q2 = q_ref[0, q0:q1, :]
                in_head = (lane >= hh * HEAD_DIM) & (lane < (hh + 1) * HEAD_DIM)
                qh = jnp.where(in_head, q2, jnp.zeros_like(q2))
                k_off = k_ref[0, 0:q0, :] if i else None
                k_diag = k_ref[0, q0:q1, :]
            else:
                qh = q_ref[0, q0:q1, hh * SLOT:(hh + 1) * SLOT]
                k_off = k_ref[0, 0:q0, hh * SLOT:(hh + 1) * SLOT] if i else None
                k_diag = k_ref[0, q0:q1, hh * SLOT:(hh + 1) * SLOT]
            s_diag = _dot_nt(qh, k_diag)
            s_off = _dot_nt(qh, k_off) if i else None
            if decay:
                head = 2 * pair + hh
                cblk = c_ref[0, q0:q1, :]
                ci = jnp.sum(jnp.where(lane == head, cblk, 0.0), axis=-1, keepdims=True)
                s_diag = s_diag + (ci - ct_ref[0, pl.ds(head, 1), q0:q1])
                if i:
                    s_off = s_off + (ci - ct_ref[0, pl.ds(head, 1), 0:q0])
            s_diag = jnp.where(diag_ok, s_diag, NEG_INF)
            if i:
                o = _softmax_pv([s_off, s_diag], [v_ref[0, 0:q0, :], v_ref[0, q0:q1, :]])
            else:
                o = _softmax_pv([s_diag], [v_ref[0, q0:q1, :]])
            heads_out.append(o)
        o_ref[0, q0:q1, :] = jnp.where(lane < HEAD_DIM, heads_out[0], heads_out[1]).astype(BF16)


def _prefix_attn(q, k, v, c=None, ct=None, *, frame_causal, pair_masked_q, name):
    batch, seq, _ = v.shape
    qk_w = LANES if pair_masked_q else 2 * SLOT
    pair_spec = lambda w: pl.BlockSpec((1, seq, w), lambda b, p: (b, 0, p))
    in_specs = [pair_spec(qk_w), pair_spec(qk_w), pair_spec(LANES)]
    args = [q, k, v]
    if c is not None:
        in_specs += [pl.BlockSpec((1, seq, LANES), lambda b, p: (b, 0, 0)),
                     pl.BlockSpec((1, HEADS, seq), lambda b, p: (b, 0, 0))]
        args += [c, ct]
    return pl.pallas_call(
        functools.partial(_prefix_attn_kernel, frame_causal=frame_causal, decay=c is not None,
                          pair_masked_q=pair_masked_q),
        grid=(batch, HEADS // 2),
        in_specs=in_specs,
        out_specs=pair_spec(LANES),
        out_shape=jax.ShapeDtypeStruct((batch, seq, BRANCH_WIDTH), BF16),
        compiler_params=_params(2),
        name=name,
    )(*args)


BAND_BLOCKS = C_LEFT_CHUNKS * CHUNK // ATTN_TQ + 1


def _band_attn_kernel(q_ref, k_ref, v_ref, bias_ref, o_ref):
    seq = q_ref.shape[1]
    tq = ATTN_TQ
    lane = lax.broadcasted_iota(jnp.int32, (1, LANES), 1)
    for t in range(seq // tq):
        kb0 = max(0, t - (BAND_BLOCKS - 1))
        k0, k1 = kb0 * tq, (t + 1) * tq
        boff = BAND_BLOCKS * tq - (k1 - k0)
        q2 = q_ref[0, t * tq:(t + 1) * tq, :]
        k2 = k_ref[0, k0:k1, :]
        v2 = v_ref[0, k0:k1, :]
        heads_out = []
        for hh in range(2):
            in_head = (lane >= hh * HEAD_DIM) & (lane < (hh + 1) * HEAD_DIM)
            qh = jnp.where(in_head, q2, jnp.zeros_like(q2))
            s = _dot_nt(qh, k2) + bias_ref[hh, :, boff:]
            heads_out.append(_softmax_pv([s], [v2]))
        o_ref[0, t * tq:(t + 1) * tq, :] = jnp.where(lane < HEAD_DIM, heads_out[0], heads_out[1]).astype(BF16)


def _band_attn(q, k, v, bias):
    batch, seq, _ = v.shape
    pair_spec = pl.BlockSpec((1, seq, LANES), lambda b, p: (b, 0, p))
    return pl.pallas_call(
        _band_attn_kernel,
        grid=(batch, HEADS // 2),
        in_specs=[pair_spec, pair_spec, pair_spec,
                  pl.BlockSpec((2, ATTN_TQ, BAND_BLOCKS * ATTN_TQ), lambda b, p: (p, 0, 0))],
        out_specs=pair_spec,
        out_shape=jax.ShapeDtypeStruct((batch, seq, BRANCH_WIDTH), BF16),
        compiler_params=_params(2),
        name="band_attn",
    )(q, k, v, bias)


def _merge_kernel(x_ref, ya_ref, yb_ref, yc_ref, gmix_ref, wg_ref, bg_ref, wb_ref, wo_ref, o_ref):
    x = x_ref[...]
    h = _row_rms(x, gmix_ref[...]).astype(BF16)
    merged = None
    for n, y_ref in enumerate((ya_ref, yb_ref, yc_ref)):
        cols = slice(n * D_MODEL, (n + 1) * D_MODEL)
        gate = jax.nn.sigmoid(_dot(h, wg_ref[:, cols]) + bg_ref[:, cols])
        term = gate * _dot(y_ref[...], wb_ref[n])
        merged = term if merged is None else merged + term
    o_ref[...] = x + _dot(merged.astype(BF16), wo_ref[...])


def _merge(x, ya, yb, yc, gmix, wg, bg, wb, wo):
    t = x.shape[0]
    tm = TOKEN_TILE
    row = lambda width: pl.BlockSpec((tm, width), lambda i: (i, 0))
    consts = [gmix, wg, bg, wb, wo]
    return pl.pallas_call(
        _merge_kernel,
        grid=(t // tm,),
        in_specs=[row(D_MODEL)] + [row(BRANCH_WIDTH)] * 3 + [_const_spec(c.shape) for c in consts],
        out_specs=row(D_MODEL),
        out_shape=jax.ShapeDtypeStruct((t, D_MODEL), F32),
        compiler_params=_params(1),
        name="merge",
    )(x, ya, yb, yc, *consts)


FF_CHUNK = 1024


def _ffn_kernel(x_ref, g_ref, w1_ref, w2_ref, o_ref):
    x = x_ref[...]
    h = _row_rms(x, g_ref[...]).astype(BF16)
    acc = x
    for c in range(D_FF // FF_CHUNK):
        a = jnp.maximum(_dot(h, w1_ref[:, c * FF_CHUNK:(c + 1) * FF_CHUNK]), 0.0)
        acc = acc + _dot((a * a).astype(BF16), w2_ref[c * FF_CHUNK:(c + 1) * FF_CHUNK, :])
    o_ref[...] = acc


def _ffn(x, g, w1, w2):
    t = x.shape[0]
    tm = TOKEN_TILE
    row = pl.BlockSpec((tm, D_MODEL), lambda i: (i, 0))
    return pl.pallas_call(
        _ffn_kernel,
        grid=(t // tm,),
        in_specs=[row, _const_spec(g.shape), _const_spec(w1.shape), _const_spec(w2.shape)],
        out_specs=row,
        out_shape=jax.ShapeDtypeStruct((t, D_MODEL), F32),
        compiler_params=_params(1),
        name="ffn",
    )(x, g, w1, w2)


def _group_matrix(groups):
    ids = np.asarray(groups)
    return jnp.asarray((ids[:, None] == ids[None, :]) & (ids[:, None] >= 0), dtype=BF16)


def _band_bias(rel_bias):
    width = BAND_BLOCKS * ATTN_TQ
    r = np.arange(ATTN_TQ)[:, None]
    kpos = np.arange(width)[None, :] - (width - ATTN_TQ)
    rel = np.clip(r - kpos, -REL_CLIP, REL_CLIP) + REL_CLIP
    dchunk = r // CHUNK - kpos // CHUNK
    valid = (dchunk >= 0) & (dchunk <= C_LEFT_CHUNKS)
    return jnp.where(jnp.asarray(valid)[None], rel_bias.astype(F32)[:, rel], NEG_INF)


def _rope_tables(seq):
    half = B_ROPE // 2
    inv = ROPE_THETA ** (-jnp.arange(half, dtype=F32) / half)
    ang = jnp.arange(seq).astype(F32)[:, None] * inv[None, :]
    cos, sin = jnp.cos(ang), jnp.sin(ang)
    pad_l = jnp.ones((seq, MISC_KR), F32)
    pad_r = jnp.ones((seq, SLOT - MISC_KR - B_ROPE), F32)
    cos_t = jnp.concatenate([pad_l, cos, cos, pad_r], axis=1)
    sin_t = jnp.concatenate([0 * pad_l, -sin, sin, 0 * pad_r], axis=1)
    return cos_t, sin_t


def kernel(x, norm_mix, w_in, b_forget, b_gate, qk_norm_a, mla_q_norm, mla_kv_norm, w_q_up, w_kv_up,
           qk_norm_b_nope, qk_norm_b_rope, qk_norm_c, rel_bias, w_branch, w_out, norm_ffn, w_ff1, w_ff2):
    batch, seq, _ = x.shape
    depth = w_in.shape[0]
    o = IN_OFFS
    zeros = lambda *s: jnp.zeros(s, F32)

    misc_w = jnp.concatenate([w_in[:, :, o[3]:o[4]], zeros(depth, D_MODEL, MISC_KR - HEADS),
                              w_in[:, :, o[6]:o[7]], zeros(depth, D_MODEL, SLOT - MISC_KR - B_ROPE)], axis=2)
    w_pack = jnp.concatenate([w_in[:, :, o[0]:o[3]], w_in[:, :, o[7]:o[10]], w_in[:, :, o[4]:o[6]], misc_w],
                             axis=2).astype(BF16)
    w_gate = w_in[:, :, o[10]:o[11]].astype(BF16)

    attn_scale = HEAD_DIM ** -0.5
    tile8 = lambda g: jnp.tile(g, (1, HEADS))
    hgain = jnp.stack([tile8(qk_norm_a[:, 0]) * attn_scale, tile8(qk_norm_a[:, 1]),
                       tile8(qk_norm_c[:, 0]) * attn_scale, tile8(qk_norm_c[:, 1])], axis=1)

    mla_scale = (B_NOPE + B_ROPE) ** -0.5
    pad = SLOT - B_NOPE - B_ROPE
    wq_slot = jnp.pad(w_q_up.reshape(depth, B_Q_LORA, HEADS, B_NOPE + B_ROPE),
                      ((0, 0), (0, 0), (0, 0), (0, pad))).reshape(depth, B_Q_LORA, HEADS * SLOT).astype(BF16)
    wkv = w_kv_up.reshape(depth, B_KV_LORA, HEADS, 2 * B_NOPE)
    wk_slot = jnp.pad(wkv[..., :B_NOPE], ((0, 0), (0, 0), (0, 0), (0, SLOT - B_NOPE))
                      ).reshape(depth, B_KV_LORA, HEADS * SLOT).astype(BF16)
    wv = wkv[..., B_NOPE:].reshape(depth, B_KV_LORA, BRANCH_WIDTH).astype(BF16)
    gq_slot = tile8(jnp.concatenate([qk_norm_b_nope[:, 0], qk_norm_b_rope[:, 0], zeros(depth, pad)], axis=1)
                    ) * mla_scale
    gk_slot = tile8(jnp.concatenate([qk_norm_b_nope[:, 1], zeros(depth, SLOT - B_NOPE)], axis=1))
    gkr_slot = jnp.concatenate([zeros(depth, MISC_KR), qk_norm_b_rope[:, 1], zeros(depth, pad)], axis=1)
    slot_groups = np.concatenate([np.zeros(B_NOPE), np.ones(B_ROPE), -np.ones(pad)]).astype(np.int64)
    slot_groups2 = np.concatenate([slot_groups, np.where(slot_groups >= 0, slot_groups + 2, -1)])
    gslot = _group_matrix(slot_groups2)
    invcnt = jnp.asarray(np.tile(np.concatenate([np.full(B_NOPE, 1.0 / B_NOPE), np.full(B_ROPE, 1.0 / B_ROPE),
                                                  np.ones(pad)]), 2)[None, :], dtype=F32)
    g64 = _group_matrix(np.arange(MXU_DIM) // HEAD_DIM)
    tri = jnp.asarray(np.tril(np.ones((MXU_DIM, MXU_DIM))), dtype=BF16)
    cos_t, sin_t = _rope_tables(seq)
    bf_rows = jnp.pad(b_forget.astype(F32), ((0, 0), (0, LANES - HEADS)))

    w_branch_b = w_branch.astype(BF16)
    w_out_b = w_out.astype(BF16)
    w_ff1_b = w_ff1.astype(BF16)
    w_ff2_b = w_ff2.astype(BF16)

    xt = x.reshape(batch * seq, D_MODEL)
    seq3 = lambda a: a.reshape(batch, seq, a.shape[-1])
    for l in range(depth):
        qa, ka, va, qc, kc, vc, qdn, kvdn, misc = _in_proj(
            xt, norm_mix[l][None], w_pack[l], g64, hgain[l], mla_q_norm[l][None], mla_kv_norm[l][None])
        c, ct = _fox_prep(misc, bf_rows[l][None], tri, batch, seq)
        qb, kb, vb = _mla_prep(qdn, kvdn, misc, cos_t, sin_t, wq_slot[l], wk_slot[l], wv[l], gslot, invcnt,
                               gq_slot[l][None], gk_slot[l][None], gkr_slot[l][None], seq)
        ya = _prefix_attn(seq3(qa), seq3(ka), seq3(va), c, ct, frame_causal=True, pair_masked_q=True,
                          name="fox_attn")
        yb = _prefix_attn(seq3(qb), seq3(kb), seq3(vb), frame_causal=False, pair_masked_q=False,
                          name="mla_attn")
        yc = _band_attn(seq3(qc), seq3(kc), seq3(vc), _band_bias(rel_bias[l]))
        flat = lambda a: a.reshape(batch * seq, BRANCH_WIDTH)
        xt = _merge(xt, flat(ya), flat(yb), flat(yc), norm_mix[l][None], w_gate[l], b_gate[l][None],
                    w_branch_b[l], w_out_b[l])
        xt = _ffn(xt, norm_ffn[l][None], w_ff1_b[l], w_ff2_b[l])
    return xt.reshape(batch, seq, D_MODEL)
```

```python
import functools

import numpy as np
import jax
import jax.numpy as jnp
from jax import lax
from jax.experimental import pallas as pl
from jax.experimental.pallas import tpu as pltpu

F32 = jnp.float32
BF16 = jnp.bfloat16

D_MODEL = 1024
DEPTH = 4
CHUNK = 64
EPS = 1e-6
NEG_INF = -1e30
HEADS = 8
HEAD_DIM = 64
B_NOPE = 64
B_ROPE = 32
B_Q_LORA = 384
B_KV_LORA = 256
ROPE_THETA = 10000.0
C_LEFT_CHUNKS = 8
REL_CLIP = 256
N_BRANCH = 3
BRANCH_WIDTH = 512
D_FF = 4 * D_MODEL
IN_SIZES = (512, 512, 512, HEADS, B_Q_LORA, B_KV_LORA, B_ROPE, 512, 512, 512, N_BRANCH * D_MODEL)
IN_OFFS = tuple(int(v) for v in np.cumsum((0,) + IN_SIZES))

LANES = 128
MXU_DIM = 256
SLOT = LANES
TOKEN_TILE = 512
ATTN_TQ = 256
VMEM_LIMIT = 56 * 1024 * 1024

PACK_QD = 6 * 512
PACK_KVD = PACK_QD + B_Q_LORA
PACK_MISC = PACK_KVD + B_KV_LORA
PACK_WIDTH = PACK_MISC + LANES
MISC_KR = 64
LOG2E = float(np.log2(np.e))


def _dot(a, b):
    return jnp.dot(a, b, preferred_element_type=F32)


def _dot_nt(a, b):
    return lax.dot_general(a, b, (((1,), (1,)), ((), ())), preferred_element_type=F32)


def _split2(v):
    hi = v.astype(BF16)
    lo = (v - hi.astype(F32)).astype(BF16)
    return hi, lo


def _group_sumsq(u, g_ref):
    hi, lo = _split2(u * u)
    g = g_ref[...]
    return _dot(hi, g) + _dot(lo, g)


def _row_rms(x, gain):
    ms = jnp.mean(x * x, axis=-1, keepdims=True)
    return x * lax.rsqrt(ms + EPS) * gain


def _const_spec(shape):
    nd = len(shape)
    return pl.BlockSpec(shape, lambda *_: (0,) * nd)


def _params(n_axes):
    return pltpu.CompilerParams(dimension_semantics=("parallel",) * n_axes,
                                vmem_limit_bytes=VMEM_LIMIT)


def _in_proj_kernel(x_ref, gmix_ref, w_ref, g64_ref, hgain_ref, gq_ref, gkv_ref,
                    qa_ref, ka_ref, va_ref, qc_ref, kc_ref, vc_ref, qd_ref, kvd_ref, misc_ref):
    h = _row_rms(x_ref[...], gmix_ref[...]).astype(BF16)

    def head_normed(col, gain_row):
        u = _dot(h, w_ref[:, col:col + 512])
        halves = []
        for j in range(2):
            ub = u[:, MXU_DIM * j:MXU_DIM * (j + 1)]
            ss = _group_sumsq(ub, g64_ref)
            halves.append(ub * lax.rsqrt(ss * (1.0 / HEAD_DIM) + EPS))
        return (jnp.concatenate(halves, axis=1) * hgain_ref[gain_row:gain_row + 1, :]).astype(BF16)

    qa_ref[...] = head_normed(0, 0)
    ka_ref[...] = head_normed(512, 1)
    va_ref[...] = _dot(h, w_ref[:, 1024:1536]).astype(BF16)
    qc_ref[...] = head_normed(1536, 2)
    kc_ref[...] = head_normed(2048, 3)
    vc_ref[...] = _dot(h, w_ref[:, 2560:3072]).astype(BF16)
    qd_ref[...] = _row_rms(_dot(h, w_ref[:, PACK_QD:PACK_KVD]), gq_ref[...]).astype(BF16)
    kvd_ref[...] = _row_rms(_dot(h, w_ref[:, PACK_KVD:PACK_MISC]), gkv_ref[...]).astype(BF16)
    misc_ref[...] = _dot(h, w_ref[:, PACK_MISC:PACK_WIDTH])


def _in_proj(x, gmix, w_pack, g64, hgain, gq, gkv):
    t = x.shape[0]
    tm = TOKEN_TILE
    row = lambda width: pl.BlockSpec((tm, width), lambda i: (i, 0))
    out_shapes = [jax.ShapeDtypeStruct((t, 512), BF16)] * 6 + [
        jax.ShapeDtypeStruct((t, B_Q_LORA), BF16),
        jax.ShapeDtypeStruct((t, B_KV_LORA), BF16),
        jax.ShapeDtypeStruct((t, LANES), F32)]
    return pl.pallas_call(
        _in_proj_kernel,
        grid=(t // tm,),
        in_specs=[row(D_MODEL), _const_spec(gmix.shape), _const_spec(w_pack.shape),
                  _const_spec(g64.shape), _const_spec(hgain.shape), _const_spec(gq.shape),
                  _const_spec(gkv.shape)],
        out_specs=[row(512)] * 6 + [row(B_Q_LORA), row(B_KV_LORA), row(LANES)],
        out_shape=out_shapes,
        compiler_params=_params(1),
        name="in_proj",
    )(x, gmix, w_pack, g64, hgain, gq, gkv)


def _split3(v):
    hi = v.astype(BF16)
    r1 = v - hi.astype(F32)
    mid = r1.astype(BF16)
    lo = (r1 - mid.astype(F32)).astype(BF16)
    return hi, mid, lo


def _fox_prep_kernel(misc_ref, bf_ref, tri_ref, pq_ref, pk_ref, oq_ref, ok_ref, aq_ref, ak_ref):
    blk = MXU_DIM
    seq = misc_ref.shape[0]
    lane = lax.broadcasted_iota(jnp.int32, (blk, LANES), 1)
    tri = tri_ref[...]
    carry = jnp.zeros((1, LANES), F32)
    for j in range(seq // blk):
        rows = slice(j * blk, (j + 1) * blk)
        z = misc_ref[rows, :] + bf_ref[...]
        lf = -(jnp.maximum(-z, 0.0) + jnp.log1p(jnp.exp(-jnp.abs(z))))
        lf = jnp.where(lane < HEADS, lf, 0.0)
        hi, mid, lo = _split3(lf)
        c = _dot(tri, hi) + _dot(tri, mid) + _dot(tri, lo) + carry
        carry = c[blk - 1:blk, :]
        c3 = jnp.concatenate(_split3(c * LOG2E), axis=1)
        aq = _dot(c3, pq_ref[...]) + oq_ref[...]
        ak = _dot(c3, pk_ref[...]) + ok_ref[...]
        for h in range(HEADS):
            aq_ref[0, h, rows, :] = aq[:, h * LANES:(h + 1) * LANES].astype(BF16)
            ak_ref[0, h, rows, :] = ak[:, h * LANES:(h + 1) * LANES].astype(BF16)


def _fox_prep(misc, bf_row, tri, batch, seq):
    pq = np.zeros((3 * LANES, HEADS * LANES), np.float32)
    pk = np.zeros((3 * LANES, HEADS * LANES), np.float32)
    oq = np.zeros((1, HEADS * LANES), np.float32)
    ok = np.zeros((1, HEADS * LANES), np.float32)
    for h in range(HEADS):
        for t in range(3):
            pq[t * LANES + h, h * LANES + t] = 1.0
            pk[t * LANES + h, h * LANES + 3 + t] = -1.0
            oq[0, h * LANES + 3 + t] = 1.0
            ok[0, h * LANES + t] = 1.0
    consts = [bf_row, tri, jnp.asarray(pq, BF16), jnp.asarray(pk, BF16), jnp.asarray(oq), jnp.asarray(ok)]
    aug_spec = pl.BlockSpec((1, HEADS, seq, LANES), lambda b: (b, 0, 0, 0))
    aug_shape = jax.ShapeDtypeStruct((batch, HEADS, seq, LANES), BF16)
    return pl.pallas_call(
        _fox_prep_kernel,
        grid=(batch,),
        in_specs=[pl.BlockSpec((seq, LANES), lambda b: (b, 0))] + [_const_spec(c.shape) for c in consts],
        out_specs=[aug_spec, aug_spec],
        out_shape=[aug_shape, aug_shape],
        compiler_params=_params(1),
        name="fox_prep",
    )(misc, *consts)


def _rope_slot(x, cos, sin, lo_half):
    partner = jnp.where(lo_half, pltpu.roll(x, LANES - B_ROPE // 2, 1), pltpu.roll(x, B_ROPE // 2, 1))
    return x * cos + partner * sin


def _mla_prep_kernel(qd_ref, kvd_ref, misc_ref, cos_ref, sin_ref, wq_ref, wk_ref, wv_ref,
                     gslot_ref, invcnt_ref, gq_ref, gk_ref, gkr_ref, q_ref, k_ref, v_ref):
    tm = qd_ref.shape[0]
    lane = lax.broadcasted_iota(jnp.int32, (tm, SLOT), 1)
    lo_half = lane < MISC_KR + B_ROPE // 2
    in_kr = (lane >= MISC_KR) & (lane < MISC_KR + B_ROPE)
    cos = cos_ref[...]
    sin = sin_ref[...]
    qd = qd_ref[...]
    kvd = kvd_ref[...]

    def slot_normed(u, gain_ref, col):
        ss = _group_sumsq(u, gslot_ref)
        return u * lax.rsqrt(ss * invcnt_ref[...] + EPS) * gain_ref[:, col:col + MXU_DIM]

    m = misc_ref[...]
    ss_kr = jnp.sum(jnp.where(in_kr, m * m, 0.0), axis=-1, keepdims=True)
    kr = _rope_slot(m * lax.rsqrt(ss_kr * (1.0 / B_ROPE) + EPS) * gkr_ref[...], cos, sin, lo_half)

    for j in range(HEADS * SLOT // MXU_DIM):
        col = j * MXU_DIM
        qn = slot_normed(_dot(qd, wq_ref[:, col:col + MXU_DIM]), gq_ref, col)
        kn = slot_normed(_dot(kvd, wk_ref[:, col:col + MXU_DIM]), gk_ref, col)
        for s in range(MXU_DIM // SLOT):
            sl = slice(s * SLOT, (s + 1) * SLOT)
            q_ref[:, col + s * SLOT:col + (s + 1) * SLOT] = _rope_slot(qn[:, sl], cos, sin, lo_half).astype(BF16)
            k_ref[:, col + s * SLOT:col + (s + 1) * SLOT] = (kn[:, sl] + kr).astype(BF16)
    v_ref[...] = _dot(kvd, wv_ref[...]).astype(BF16)


def _mla_prep(qdn, kvdn, misc, cos_t, sin_t, wq, wk, wv, gslot, invcnt, gq, gk, gkr, seq):
    t = qdn.shape[0]
    tm = TOKEN_TILE
    per_seq = seq // tm
    row = lambda width: pl.BlockSpec((tm, width), lambda i: (i, 0))
    pos = pl.BlockSpec((tm, SLOT), lambda i: (i % per_seq, 0))
    consts = [wq, wk, wv, gslot, invcnt, gq, gk, gkr]
    return pl.pallas_call(
        _mla_prep_kernel,
        grid=(t // tm,),
        in_specs=[row(B_Q_LORA), row(B_KV_LORA), row(LANES), pos, pos] + [_const_spec(c.shape) for c in consts],
        out_specs=[row(HEADS * SLOT), row(HEADS * SLOT), row(BRANCH_WIDTH)],
        out_shape=[jax.ShapeDtypeStruct((t, HEADS * SLOT), BF16),
                   jax.ShapeDtypeStruct((t, HEADS * SLOT), BF16),
                   jax.ShapeDtypeStruct((t, BRANCH_WIDTH), BF16)],
        compiler_params=_params(1),
        name="mla_prep",
    )(qdn, kvdn, misc, cos_t, sin_t, *consts)


def _softmax_pv(score_blocks, v_blocks):
    m = functools.reduce(jnp.maximum, [jnp.max(s, axis=-1, keepdims=True) for s in score_blocks])
    ps = [jnp.exp2(s - m) for s in score_blocks]
    l = functools.reduce(jnp.add, [jnp.sum(p, axis=-1, keepdims=True) for p in ps])
    o = functools.reduce(jnp.add, [_dot(p.astype(BF16), v) for p, v in zip(ps, v_blocks)])
    return o / l


def _head_lanes(lane, hh):
    return (lane >= hh * HEAD_DIM) & (lane < (hh + 1) * HEAD_DIM)


def _prefix_attn_kernel(*refs, frame_causal, decay):
    if decay:
        q_ref, k_ref, v_ref, aq_ref, ak_ref, o_ref, kx_ref = refs
        for hh in range(2):
            kx_ref[hh, :, 0:LANES] = k_ref[0]
            kx_ref[hh, :, LANES:2 * LANES] = ak_ref[0, hh]
    else:
        q_ref, k_ref, v_ref, o_ref = refs
    seq = q_ref.shape[1]
    tq = ATTN_TQ
    lane = lax.broadcasted_iota(jnp.int32, (1, LANES), 1)
    row = lax.broadcasted_iota(jnp.int32, (tq, tq), 0)
    col = lax.broadcasted_iota(jnp.int32, (tq, tq), 1)
    if frame_causal:
        diag_ok = col <= row
    else:
        diag_ok = (col // CHUNK) <= (row // CHUNK)
    for i in range(seq // tq):
        q0, q1 = i * tq, (i + 1) * tq
        heads_out = []
        for hh in range(2):
            if decay:
                q2 = q_ref[0, q0:q1, :]
                qh = jnp.concatenate([jnp.where(_head_lanes(lane, hh), q2, jnp.zeros_like(q2)),
                                      aq_ref[0, hh, q0:q1, :]], axis=1)
                k_of = lambda a, b: kx_ref[hh, a:b, :]
            else:
                qh = q_ref[0, q0:q1, hh * SLOT:(hh + 1) * SLOT]
                k_of = lambda a, b: k_ref[0, a:b, hh * SLOT:(hh + 1) * SLOT]
            s_diag = jnp.where(diag_ok, _dot_nt(qh, k_of(q0, q1)), NEG_INF)
            if i:
                o = _softmax_pv([_dot_nt(qh, k_of(0, q0)), s_diag], [v_ref[0, 0:q0, :], v_ref[0, q0:q1, :]])
            else:
                o = _softmax_pv([s_diag], [v_ref[0, q0:q1, :]])
            heads_out.append(o)
        o_ref[0, q0:q1, :] = jnp.where(lane < HEAD_DIM, heads_out[0], heads_out[1]).astype(BF16)


def _prefix_attn(q, k, v, aq=None, ak=None, *, frame_causal, name):
    batch, seq, _ = v.shape
    decay = aq is not None
    qk_w = LANES if decay else 2 * SLOT
    pair_spec = lambda w: pl.BlockSpec((1, seq, w), lambda b, p: (b, 0, p))
    in_specs = [pair_spec(qk_w), pair_spec(qk_w), pair_spec(LANES)]
    args = [q, k, v]
    scratch = []
    if decay:
        aug_spec = pl.BlockSpec((1, 2, seq, LANES), lambda b, p: (b, p, 0, 0))
        in_specs += [aug_spec, aug_spec]
        args += [aq, ak]
        scratch = [pltpu.VMEM((2, seq, 2 * LANES), BF16)]
    return pl.pallas_call(
        functools.partial(_prefix_attn_kernel, frame_causal=frame_causal, decay=decay),
        grid=(batch, HEADS // 2),
        in_specs=in_specs,
        out_specs=pair_spec(LANES),
        out_shape=jax.ShapeDtypeStruct((batch, seq, BRANCH_WIDTH), BF16),
        scratch_shapes=scratch,
        compiler_params=_params(2),
        name=name,
    )(*args)


BAND_BLOCKS = C_LEFT_CHUNKS * CHUNK // ATTN_TQ + 1
BAND_WIDTH = BAND_BLOCKS * ATTN_TQ
BAND_PERIOD = BAND_WIDTH + ATTN_TQ


def _band_attn_kernel(q_ref, k_ref, v_ref, w_ref, o_ref, bias_ref):
    seq = q_ref.shape[1]
    tq = ATTN_TQ
    lane = lax.broadcasted_iota(jnp.int32, (1, LANES), 1)

    @pl.when(pl.program_id(1) == 0)
    def _():
        r = lax.broadcasted_iota(jnp.int32, (tq, BAND_WIDTH), 0)
        c = lax.broadcasted_iota(jnp.int32, (tq, BAND_WIDTH), 1)
        dchunk = r // CHUNK - (c // CHUNK - C_LEFT_CHUNKS)
        valid = (dchunk >= 0) & (dchunk <= C_LEFT_CHUNKS)
        for hh in range(2):
            table = jnp.broadcast_to(w_ref[hh], (tq, BAND_PERIOD))
            toeplitz = pltpu.roll(table, 0, 1, stride=1, stride_axis=0)
            bias_ref[hh] = jnp.where(valid, toeplitz[:, :BAND_WIDTH], NEG_INF)

    for t in range(seq // tq):
        kb0 = max(0, t - (BAND_BLOCKS - 1))
        k0, k1 = kb0 * tq, (t + 1) * tq
        boff = BAND_WIDTH - (k1 - k0)
        q2 = q_ref[0, t * tq:(t + 1) * tq, :]
        k2 = k_ref[0, k0:k1, :]
        v2 = v_ref[0, k0:k1, :]
        heads_out = []
        for hh in range(2):
            qh = jnp.where(_head_lanes(lane, hh), q2, jnp.zeros_like(q2))
            s = _dot_nt(qh, k2) + bias_ref[hh, :, boff:]
            heads_out.append(_softmax_pv([s], [v2]))
        o_ref[0, t * tq:(t + 1) * tq, :] = jnp.where(lane < HEAD_DIM, heads_out[0], heads_out[1]).astype(BF16)


def _band_attn(q, k, v, w):
    batch, seq, _ = v.shape
    pair_spec = pl.BlockSpec((1, seq, LANES), lambda p, b: (b, 0, p))
    return pl.pallas_call(
        _band_attn_kernel,
        grid=(HEADS // 2, batch),
        in_specs=[pair_spec, pair_spec, pair_spec,
                  pl.BlockSpec((2, 1, BAND_PERIOD), lambda p, b: (p, 0, 0))],
        out_specs=pair_spec,
        out_shape=jax.ShapeDtypeStruct((batch, seq, BRANCH_WIDTH), BF16),
        scratch_shapes=[pltpu.VMEM((2, ATTN_TQ, BAND_WIDTH), F32)],
        compiler_params=pltpu.CompilerParams(dimension_semantics=("parallel", "arbitrary"),
                                             vmem_limit_bytes=VMEM_LIMIT),
        name="band_attn",
    )(q, k, v, w)


def _merge_kernel(x_ref, ya_ref, yb_ref, yc_ref, gmix_ref, wg_ref, bg_ref, wb_ref, wo_ref, o_ref):
    x = x_ref[...]
    h = _row_rms(x, gmix_ref[...]).astype(BF16)
    merged = None
    for n, y_ref in enumerate((ya_ref, yb_ref, yc_ref)):
        cols = slice(n * D_MODEL, (n + 1) * D_MODEL)
        gate = jax.nn.sigmoid(_dot(h, wg_ref[:, cols]) + bg_ref[:, cols])
        term = gate * _dot(y_ref[...], wb_ref[n])
        merged = term if merged is None else merged + term
    o_ref[...] = x + _dot(merged.astype(BF16), wo_ref[...])


def _merge(x, ya, yb, yc, gmix, wg, bg, wb, wo):
    t = x.shape[0]
    tm = TOKEN_TILE
    row = lambda width: pl.BlockSpec((tm, width), lambda i: (i, 0))
    consts = [gmix, wg, bg, wb, wo]
    return pl.pallas_call(
        _merge_kernel,
        grid=(t // tm,),
        in_specs=[row(D_MODEL)] + [row(BRANCH_WIDTH)] * 3 + [_const_spec(c.shape) for c in consts],
        out_specs=row(D_MODEL),
        out_shape=jax.ShapeDtypeStruct((t, D_MODEL), F32),
        compiler_params=_params(1),
        name="merge",
    )(x, ya, yb, yc, *consts)


FF_CHUNK = 1024


def _ffn_kernel(x_ref, g_ref, w1_ref, w2_ref, o_ref):
    x = x_ref[...]
    h = _row_rms(x, g_ref[...]).astype(BF16)
    acc = x
    for c in range(D_FF // FF_CHUNK):
        a = jnp.maximum(_dot(h, w1_ref[:, c * FF_CHUNK:(c + 1) * FF_CHUNK]), 0.0)
        acc = acc + _dot((a * a).astype(BF16), w2_ref[c * FF_CHUNK:(c + 1) * FF_CHUNK, :])
    o_ref[...] = acc


def _ffn(x, g, w1, w2):
    t = x.shape[0]
    tm = TOKEN_TILE
    row = pl.BlockSpec((tm, D_MODEL), lambda i: (i, 0))
    return pl.pallas_call(
        _ffn_kernel,
        grid=(t // tm,),
        in_specs=[row, _const_spec(g.shape), _const_spec(w1.shape), _const_spec(w2.shape)],
        out_specs=row,
        out_shape=jax.ShapeDtypeStruct((t, D_MODEL), F32),
        compiler_params=_params(1),
        name="ffn",
    )(x, g, w1, w2)


def _group_matrix(groups):
    ids = np.asarray(groups)
    return jnp.asarray((ids[:, None] == ids[None, :]) & (ids[:, None] >= 0), dtype=BF16)


def _band_table(rel_bias):
    rb = rel_bias.astype(F32) * LOG2E
    left = BAND_WIDTH - ATTN_TQ - REL_CLIP
    far = jnp.broadcast_to(rb[..., 2 * REL_CLIP:], rb.shape[:-1] + (BAND_PERIOD,))
    w = jnp.concatenate([far[..., :left], rb[..., ::-1], far[..., left + 2 * REL_CLIP + 1:]], axis=-1)
    return w[..., None, :]


def _rope_tables(seq):
    half = B_ROPE // 2
    inv = ROPE_THETA ** (-jnp.arange(half, dtype=F32) / half)
    ang = jnp.arange(seq).astype(F32)[:, None] * inv[None, :]
    cos, sin = jnp.cos(ang), jnp.sin(ang)
    pad_l = jnp.ones((seq, MISC_KR), F32)
    pad_r = jnp.ones((seq, SLOT - MISC_KR - B_ROPE), F32)
    cos_t = jnp.concatenate([pad_l, cos, cos, pad_r], axis=1)
    sin_t = jnp.concatenate([0 * pad_l, -sin, sin, 0 * pad_r], axis=1)
    return cos_t, sin_t


def kernel(x, norm_mix, w_in, b_forget, b_gate, qk_norm_a, mla_q_norm, mla_kv_norm, w_q_up, w_kv_up,
           qk_norm_b_nope, qk_norm_b_rope, qk_norm_c, rel_bias, w_branch, w_out, norm_ffn, w_ff1, w_ff2):
    batch, seq, _ = x.shape
    depth = w_in.shape[0]
    o = IN_OFFS
    zeros = lambda *s: jnp.zeros(s, F32)

    misc_w = jnp.concatenate([w_in[:, :, o[3]:o[4]], zeros(depth, D_MODEL, MISC_KR - HEADS),
                              w_in[:, :, o[6]:o[7]], zeros(depth, D_MODEL, SLOT - MISC_KR - B_ROPE)], axis=2)
    w_pack = jnp.concatenate([w_in[:, :, o[0]:o[3]], w_in[:, :, o[7]:o[10]], w_in[:, :, o[4]:o[6]], misc_w],
                             axis=2).astype(BF16)
    w_gate = w_in[:, :, o[10]:o[11]].astype(BF16)

    attn_scale = HEAD_DIM ** -0.5 * LOG2E
    tile8 = lambda g: jnp.tile(g, (1, HEADS))
    hgain = jnp.stack([tile8(qk_norm_a[:, 0]) * attn_scale, tile8(qk_norm_a[:, 1]),
                       tile8(qk_norm_c[:, 0]) * attn_scale, tile8(qk_norm_c[:, 1])], axis=1)

    mla_scale = (B_NOPE + B_ROPE) ** -0.5 * LOG2E
    pad = SLOT - B_NOPE - B_ROPE
    wq_slot = jnp.pad(w_q_up.reshape(depth, B_Q_LORA, HEADS, B_NOPE + B_ROPE),
                      ((0, 0), (0, 0), (0, 0), (0, pad))).reshape(depth, B_Q_LORA, HEADS * SLOT).astype(BF16)
    wkv = w_kv_up.reshape(depth, B_KV_LORA, HEADS, 2 * B_NOPE)
    wk_slot = jnp.pad(wkv[..., :B_NOPE], ((0, 0), (0, 0), (0, 0), (0, SLOT - B_NOPE))
                      ).reshape(depth, B_KV_LORA, HEADS * SLOT).astype(BF16)
    wv = wkv[..., B_NOPE:].reshape(depth, B_KV_LORA, BRANCH_WIDTH).astype(BF16)
    gq_slot = tile8(jnp.concatenate([qk_norm_b_nope[:, 0], qk_norm_b_rope[:, 0], zeros(depth, pad)], axis=1)
                    ) * mla_scale
    gk_slot = tile8(jnp.concatenate([qk_norm_b_nope[:, 1], zeros(depth, SLOT - B_NOPE)], axis=1))
    gkr_slot = jnp.concatenate([zeros(depth, MISC_KR), qk_norm_b_rope[:, 1], zeros(depth, pad)], axis=1)
    slot_groups = np.concatenate([np.zeros(B_NOPE), np.ones(B_ROPE), -np.ones(pad)]).astype(np.int64)
    slot_groups2 = np.concatenate([slot_groups, np.where(slot_groups >= 0, slot_groups + 2, -1)])
    gslot = _group_matrix(slot_groups2)
    invcnt = jnp.asarray(np.tile(np.concatenate([np.full(B_NOPE, 1.0 / B_NOPE), np.full(B_ROPE, 1.0 / B_ROPE),
                                                  np.ones(pad)]), 2)[None, :], dtype=F32)
    g64 = _group_matrix(np.arange(MXU_DIM) // HEAD_DIM)
    tri = jnp.asarray(np.tril(np.ones((MXU_DIM, MXU_DIM))), dtype=BF16)
    cos_t, sin_t = _rope_tables(seq)
    bf_rows = jnp.pad(b_forget.astype(F32), ((0, 0), (0, LANES - HEADS)))
    band_tables = _band_table(rel_bias)

    w_branch_b = w_branch.astype(BF16)
    w_out_b = w_out.astype(BF16)
    w_ff1_b = w_ff1.astype(BF16)
    w_ff2_b = w_ff2.astype(BF16)

    xt = x.reshape(batch * seq, D_MODEL)
    seq3 = lambda a: a.reshape(batch, seq, a.shape[-1])
    for l in range(depth):
        qa, ka, va, qc, kc, vc, qdn, kvdn, misc = _in_proj(
            xt, norm_mix[l][None], w_pack[l], g64, hgain[l], mla_q_norm[l][None], mla_kv_norm[l][None])
        aq, ak = _fox_prep(misc, bf_rows[l][None], tri, batch, seq)
        qb, kb, vb = _mla_prep(qdn, kvdn, misc, cos_t, sin_t, wq_slot[l], wk_slot[l], wv[l], gslot, invcnt,
                               gq_slot[l][None], gk_slot[l][None], gkr_slot[l][None], seq)
        ya = _prefix_attn(seq3(qa), seq3(ka), seq3(va), aq, ak, frame_causal=True, name="fox_attn")
        yb = _prefix_attn(seq3(qb), seq3(kb), seq3(vb), frame_causal=False, name="mla_attn")
        yc = _band_attn(seq3(qc), seq3(kc), seq3(vc), band_tables[l])
        flat = lambda a: a.reshape(batch * seq, BRANCH_WIDTH)
        xt = _merge(xt, flat(ya), flat(yb), flat(yc), norm_mix[l][None], w_gate[l], b_gate[l][None],
                    w_branch_b[l], w_out_b[l])
        xt = _ffn(xt, norm_ffn[l][None], w_ff1_b[l], w_ff2_b[l])
    return xt.reshape(batch, seq, D_MODEL)
```

```python
import functools
from typing import NamedTuple

import numpy as np
import jax
import jax.numpy as jnp
from jax import lax
from jax.experimental import pallas as pl
from jax.experimental.pallas import tpu as pltpu

F32 = jnp.float32
BF16 = jnp.bfloat16

D_MODEL = 1024
DEPTH = 4
CHUNK = 64
EPS = 1e-6
NEG_INF = -1e30
HEADS = 8
HEAD_DIM = 64
B_NOPE = 64
B_ROPE = 32
B_Q_LORA = 384
B_KV_LORA = 256
ROPE_THETA = 10000.0
C_LEFT_CHUNKS = 8
REL_CLIP = 256
N_BRANCH = 3
BRANCH_WIDTH = 512
D_FF = 4 * D_MODEL
IN_SIZES = (512, 512, 512, HEADS, B_Q_LORA, B_KV_LORA, B_ROPE, 512, 512, 512, N_BRANCH * D_MODEL)
IN_OFFS = tuple(int(v) for v in np.cumsum((0,) + IN_SIZES))

LANES = 128
MXU_DIM = 256
SLOT = LANES
TOKEN_TILE = 512
ATTN_TQ = 256
VMEM_LIMIT = 56 * 1024 * 1024

PACK_QD = 6 * 512
PACK_KVD = PACK_QD + B_Q_LORA
PACK_MISC = PACK_KVD + B_KV_LORA
PACK_WIDTH = PACK_MISC + LANES
MISC_KR = 64
LOG2E = float(np.log2(np.e))


def _dot(a, b):
    return jnp.dot(a, b, preferred_element_type=F32)


def _dot_nt(a, b):
    return lax.dot_general(a, b, (((1,), (1,)), ((), ())), preferred_element_type=F32)


def _split2(v):
    hi = v.astype(BF16)
    lo = (v - hi.astype(F32)).astype(BF16)
    return hi, lo


def _group_sumsq(u, g_ref):
    hi, lo = _split2(u * u)
    g = g_ref[...]
    return _dot(hi, g) + _dot(lo, g)


def _row_rms(x, gain):
    ms = jnp.mean(x * x, axis=-1, keepdims=True)
    return x * lax.rsqrt(ms + EPS) * gain


class _Layer(NamedTuple):
    stacked: jax.Array
    index: int


def _const_spec(op):
    if isinstance(op, _Layer):
        tail = op.stacked.shape[1:]
        return pl.BlockSpec((None,) + tail, lambda *_: (op.index,) + (0,) * len(tail))
    return pl.BlockSpec(op.shape, lambda *_: (0,) * op.ndim)


def _arrays(ops):
    return [op.stacked if isinstance(op, _Layer) else op for op in ops]


def _params(n_axes):
    return pltpu.CompilerParams(dimension_semantics=("parallel",) * n_axes,
                                vmem_limit_bytes=VMEM_LIMIT)


def _in_proj_kernel(x_ref, gmix_ref, w_ref, hgain_ref, gq_ref, gkv_ref,
                    qa_ref, ka_ref, va_ref, qc_ref, kc_ref, vc_ref, qd_ref, kvd_ref, misc_ref):
    h = _row_rms(x_ref[...], gmix_ref[...]).astype(BF16)

    first_head = lax.broadcasted_iota(jnp.int32, (1, LANES), 1) < HEAD_DIM

    def head_normed(col, gain_row):
        u = _dot(h, w_ref[:, col:col + 512])
        parts = []
        for j in range(512 // LANES):
            ub = u[:, LANES * j:LANES * (j + 1)]
            sq = ub * ub
            s0 = jnp.sum(jnp.where(first_head, sq, 0.0), axis=-1, keepdims=True)
            s1 = jnp.sum(jnp.where(first_head, 0.0, sq), axis=-1, keepdims=True)
            parts.append(ub * lax.rsqrt(jnp.where(first_head, s0, s1) * (1.0 / HEAD_DIM) + EPS))
        return (jnp.concatenate(parts, axis=1) * hgain_ref[gain_row:gain_row + 1, :]).astype(BF16)

    qa_ref[...] = head_normed(0, 0)
    ka_ref[...] = head_normed(512, 1)
    va_ref[...] = _dot(h, w_ref[:, 1024:1536]).astype(BF16)
    qc_ref[...] = head_normed(1536, 2)
    kc_ref[...] = head_normed(2048, 3)
    vc_ref[...] = _dot(h, w_ref[:, 2560:3072]).astype(BF16)
    qd_ref[...] = _row_rms(_dot(h, w_ref[:, PACK_QD:PACK_KVD]), gq_ref[...]).astype(BF16)
    kvd_ref[...] = _row_rms(_dot(h, w_ref[:, PACK_KVD:PACK_MISC]), gkv_ref[...]).astype(BF16)
    misc_ref[...] = _dot(h, w_ref[:, PACK_MISC:PACK_WIDTH])


def _in_proj(x, gmix, w_pack, hgain, gq, gkv):
    t = x.shape[0]
    tm = TOKEN_TILE
    row = lambda width: pl.BlockSpec((tm, width), lambda i: (i, 0))
    out_shapes = [jax.ShapeDtypeStruct((t, 512), BF16)] * 6 + [
        jax.ShapeDtypeStruct((t, B_Q_LORA), BF16),
        jax.ShapeDtypeStruct((t, B_KV_LORA), BF16),
        jax.ShapeDtypeStruct((t, LANES), F32)]
    consts = [gmix, w_pack, hgain, gq, gkv]
    return pl.pallas_call(
        _in_proj_kernel,
        grid=(t // tm,),
        in_specs=[row(D_MODEL)] + [_const_spec(c) for c in consts],
        out_specs=[row(512)] * 6 + [row(B_Q_LORA), row(B_KV_LORA), row(LANES)],
        out_shape=out_shapes,
        compiler_params=_params(1),
        name="in_proj",
    )(x, *_arrays(consts))


def _split3(v):
    hi = v.astype(BF16)
    r1 = v - hi.astype(F32)
    mid = r1.astype(BF16)
    lo = (r1 - mid.astype(F32)).astype(BF16)
    return hi, mid, lo


def _fox_prep_kernel(misc_ref, bf_ref, tri_ref, pq_ref, pk_ref, oq_ref, ok_ref, aq_ref, ak_ref):
    blk = MXU_DIM
    seq = misc_ref.shape[0]
    lane = lax.broadcasted_iota(jnp.int32, (blk, LANES), 1)
    tri = tri_ref[...]
    carry = jnp.zeros((1, LANES), F32)
    for j in range(seq // blk):
        rows = slice(j * blk, (j + 1) * blk)
        z = misc_ref[rows, :] + bf_ref[...]
        lf = -(jnp.maximum(-z, 0.0) + jnp.log1p(jnp.exp(-jnp.abs(z))))
        lf = jnp.where(lane < HEADS, lf, 0.0)
        hi, mid, lo = _split3(lf)
        c = _dot(tri, hi) + _dot(tri, mid) + _dot(tri, lo) + carry
        carry = c[blk - 1:blk, :]
        c3 = jnp.concatenate(_split3(c * LOG2E), axis=1)
        aq_ref[0, rows, :] = (_dot(c3, pq_ref[...]) + oq_ref[...]).astype(BF16)
        ak_ref[0, rows, :] = (_dot(c3, pk_ref[...]) + ok_ref[...]).astype(BF16)


AUG_LANES = LANES // HEADS


def _fox_prep(misc, bf_row, tri, batch, seq):
    pq = np.zeros((3 * LANES, LANES), np.float32)
    pk = np.zeros((3 * LANES, LANES), np.float32)
    oq = np.zeros((1, LANES), np.float32)
    ok = np.zeros((1, LANES), np.float32)
    for h in range(HEADS):
        for t in range(3):
            pq[t * LANES + h, h * AUG_LANES + t] = 1.0
            pk[t * LANES + h, h * AUG_LANES + 3 + t] = -1.0
            oq[0, h * AUG_LANES + 3 + t] = 1.0
            ok[0, h * AUG_LANES + t] = 1.0
    consts = [bf_row, tri, jnp.asarray(pq, BF16), jnp.asarray(pk, BF16), jnp.asarray(oq), jnp.asarray(ok)]
    aug_spec = pl.BlockSpec((1, seq, LANES), lambda b: (b, 0, 0))
    aug_shape = jax.ShapeDtypeStruct((batch, seq, LANES), BF16)
    return pl.pallas_call(
        _fox_prep_kernel,
        grid=(batch,),
        in_specs=[pl.BlockSpec((seq, LANES), lambda b: (b, 0))] + [_const_spec(c) for c in consts],
        out_specs=[aug_spec, aug_spec],
        out_shape=[aug_shape, aug_shape],
        compiler_params=_params(1),
        name="fox_prep",
    )(misc, *consts)


def _rope_slot(x, cos, sin, lo_half):
    partner = jnp.where(lo_half, pltpu.roll(x, LANES - B_ROPE // 2, 1), pltpu.roll(x, B_ROPE // 2, 1))
    return x * cos + partner * sin


def _mla_prep_kernel(qd_ref, kvd_ref, misc_ref, cos_ref, sin_ref, wq_ref, wk_ref, wv_ref,
                     gslot_ref, invcnt_ref, gq_ref, gk_ref, gkr_ref, q_ref, k_ref, v_ref):
    tm = qd_ref.shape[0]
    lane = lax.broadcasted_iota(jnp.int32, (tm, SLOT), 1)
    lo_half = lane < MISC_KR + B_ROPE // 2
    in_kr = (lane >= MISC_KR) & (lane < MISC_KR + B_ROPE)
    cos = cos_ref[...]
    sin = sin_ref[...]
    qd = qd_ref[...]
    kvd = kvd_ref[...]

    def slot_normed(u, gain_ref, col):
        ss = _group_sumsq(u, gslot_ref)
        return u * lax.rsqrt(ss * invcnt_ref[...] + EPS) * gain_ref[:, col:col + MXU_DIM]

    m = misc_ref[...]
    ss_kr = jnp.sum(jnp.where(in_kr, m * m, 0.0), axis=-1, keepdims=True)
    kr = _rope_slot(m * lax.rsqrt(ss_kr * (1.0 / B_ROPE) + EPS) * gkr_ref[...], cos, sin, lo_half)

    for j in range(HEADS * SLOT // MXU_DIM):
        col = j * MXU_DIM
        qn = slot_normed(_dot(qd, wq_ref[:, col:col + MXU_DIM]), gq_ref, col)
        kn = slot_normed(_dot(kvd, wk_ref[:, col:col + MXU_DIM]), gk_ref, col)
        for s in range(MXU_DIM // SLOT):
            sl = slice(s * SLOT, (s + 1) * SLOT)
            q_ref[:, col + s * SLOT:col + (s + 1) * SLOT] = _rope_slot(qn[:, sl], cos, sin, lo_half).astype(BF16)
            k_ref[:, col + s * SLOT:col + (s + 1) * SLOT] = (kn[:, sl] + kr).astype(BF16)
    v_ref[...] = _dot(kvd, wv_ref[...]).astype(BF16)


def _mla_prep(qdn, kvdn, misc, cos_t, sin_t, wq, wk, wv, gslot, invcnt, gq, gk, gkr, seq):
    t = qdn.shape[0]
    tm = TOKEN_TILE
    per_seq = seq // tm
    row = lambda width: pl.BlockSpec((tm, width), lambda i: (i, 0))
    pos = pl.BlockSpec((tm, SLOT), lambda i: (i % per_seq, 0))
    consts = [wq, wk, wv, gslot, invcnt, gq, gk, gkr]
    return pl.pallas_call(
        _mla_prep_kernel,
        grid=(t // tm,),
        in_specs=[row(B_Q_LORA), row(B_KV_LORA), row(LANES), pos, pos] + [_const_spec(c) for c in consts],
        out_specs=[row(HEADS * SLOT), row(HEADS * SLOT), row(BRANCH_WIDTH)],
        out_shape=[jax.ShapeDtypeStruct((t, HEADS * SLOT), BF16),
                   jax.ShapeDtypeStruct((t, HEADS * SLOT), BF16),
                   jax.ShapeDtypeStruct((t, BRANCH_WIDTH), BF16)],
        compiler_params=_params(1),
        name="mla_prep",
    )(qdn, kvdn, misc, cos_t, sin_t, *_arrays(consts))


def _head_lanes(lane, hh):
    return (lane >= hh * HEAD_DIM) & (lane < (hh + 1) * HEAD_DIM)


def _spread_order(n):
    return list(range(0, n, 2)) + list(range(n - 1 - (n % 2), 0, -2))


def _run_units(units, s_ref, o_ref):
    depth = s_ref.shape[0]
    lane = lax.broadcasted_iota(jnp.int32, (1, LANES), 1)
    stats = {}

    def scores(n):
        pieces = units[n][0]()
        for c0, blk in pieces:
            s_ref[n % depth, :, c0:c0 + blk.shape[1]] = blk
        stats[n] = (functools.reduce(jnp.maximum, [jnp.max(blk, axis=-1, keepdims=True) for _, blk in pieces]),
                    sum(blk.shape[1] for _, blk in pieces))

    def finish(n):
        _, values, rows, hh = units[n]
        m, klen = stats.pop(n)
        p = jnp.exp2(s_ref[n % depth, :, 0:klen] - m)
        l = jnp.sum(p, axis=-1, keepdims=True)
        o = (_dot(p.astype(BF16), values(0, klen)) / l).astype(BF16)
        if hh == 0:
            o_ref[0, rows, :] = o
        else:
            o_ref[0, rows, :] = jnp.where(lane < HEAD_DIM, o_ref[0, rows, :], o)

    for n in range(min(depth, len(units))):
        scores(n)
    for n in range(len(units)):
        finish(n)
        if n + depth < len(units):
            scores(n + depth)


def _prefix_attn_kernel(*refs, frame_causal, decay):
    if decay:
        q_ref, k_ref, v_ref, aq_ref, ak_ref, o_ref, s_ref, kx_ref = refs
        kx_ref[:, 0:LANES] = k_ref[0]
        kx_ref[:, LANES:2 * LANES] = ak_ref[0]
    else:
        q_ref, k_ref, v_ref, o_ref, s_ref = refs
    seq = q_ref.shape[1]
    tq = ATTN_TQ
    lane = lax.broadcasted_iota(jnp.int32, (1, LANES), 1)
    row = lax.broadcasted_iota(jnp.int32, (tq, tq), 0)
    col = lax.broadcasted_iota(jnp.int32, (tq, tq), 1)
    if frame_causal:
        diag_ok = col <= row
    else:
        diag_ok = (col // CHUNK) <= (row // CHUNK)

    def unit(i, hh):
        q0, q1 = i * tq, (i + 1) * tq

        def score_blocks():
            if decay:
                q2 = q_ref[0, q0:q1, :]
                a2 = aq_ref[0, q0:q1, :]
                aug0 = (2 * pl.program_id(1) + hh) * AUG_LANES
                qh = jnp.concatenate(
                    [jnp.where(_head_lanes(lane, hh), q2, jnp.zeros_like(q2)),
                     jnp.where((lane >= aug0) & (lane < aug0 + AUG_LANES), a2, jnp.zeros_like(a2))], axis=1)
                k_of = lambda a, b: kx_ref[a:b, :]
            else:
                qh = q_ref[0, q0:q1, hh * SLOT:(hh + 1) * SLOT]
                k_of = lambda a, b: k_ref[0, a:b, hh * SLOT:(hh + 1) * SLOT]
            pieces = [(0, _dot_nt(qh, k_of(0, q0)))] if i else []
            return pieces + [(q0, jnp.where(diag_ok, _dot_nt(qh, k_of(q0, q1)), NEG_INF))]

        return (score_blocks, lambda a, b: v_ref[0, a:b, :], slice(q0, q1), hh)

    _run_units([unit(i, hh) for i in _spread_order(seq // tq) for hh in range(2)], s_ref, o_ref)


def _prefix_attn(q, k, v, aq=None, ak=None, *, frame_causal, name):
    batch, seq, _ = v.shape
    decay = aq is not None
    qk_w = LANES if decay else 2 * SLOT
    pair_spec = lambda w: pl.BlockSpec((1, seq, w), lambda b, p: (b, 0, p))
    in_specs = [pair_spec(qk_w), pair_spec(qk_w), pair_spec(LANES)]
    args = [q, k, v]
    scratch = [pltpu.VMEM((2, ATTN_TQ, seq), F32)]
    if decay:
        aug_spec = pl.BlockSpec((1, seq, LANES), lambda b, p: (b, 0, 0))
        in_specs += [aug_spec, aug_spec]
        args += [aq, ak]
        scratch += [pltpu.VMEM((seq, 2 * LANES), BF16)]
    return pl.pallas_call(
        functools.partial(_prefix_attn_kernel, frame_causal=frame_causal, decay=decay),
        grid=(batch, HEADS // 2),
        in_specs=in_specs,
        out_specs=pair_spec(LANES),
        out_shape=jax.ShapeDtypeStruct((batch, seq, BRANCH_WIDTH), BF16),
        scratch_shapes=scratch,
        compiler_params=_params(2),
        name=name,
    )(*args)


BAND_BLOCKS = C_LEFT_CHUNKS * CHUNK // ATTN_TQ + 1
BAND_WIDTH = BAND_BLOCKS * ATTN_TQ
BAND_PERIOD = BAND_WIDTH + ATTN_TQ


def _band_attn_kernel(q_ref, k_ref, v_ref, w_ref, o_ref, bias_ref, s_ref):
    seq = q_ref.shape[1]
    tq = ATTN_TQ
    lane = lax.broadcasted_iota(jnp.int32, (1, LANES), 1)

    @pl.when(pl.program_id(1) == 0)
    def _():
        r = lax.broadcasted_iota(jnp.int32, (tq, BAND_WIDTH), 0)
        c = lax.broadcasted_iota(jnp.int32, (tq, BAND_WIDTH), 1)
        dchunk = r // CHUNK - (c // CHUNK - C_LEFT_CHUNKS)
        valid = (dchunk >= 0) & (dchunk <= C_LEFT_CHUNKS)
        for hh in range(2):
            table = jnp.broadcast_to(w_ref[hh], (tq, BAND_PERIOD))
            toeplitz = pltpu.roll(table, 0, 1, stride=1, stride_axis=0)
            bias_ref[hh] = jnp.where(valid, toeplitz[:, :BAND_WIDTH], NEG_INF)

    def unit(t, hh):
        k0, k1 = max(0, t - (BAND_BLOCKS - 1)) * tq, (t + 1) * tq
        boff = BAND_WIDTH - (k1 - k0)

        def score_blocks():
            q2 = q_ref[0, t * tq:(t + 1) * tq, :]
            qh = jnp.where(_head_lanes(lane, hh), q2, jnp.zeros_like(q2))
            return [(0, _dot_nt(qh, k_ref[0, k0:k1, :]) + bias_ref[hh, :, boff:])]

        return (score_blocks, lambda a, b: v_ref[0, k0 + a:k0 + b, :], slice(t * tq, (t + 1) * tq), hh)

    _run_units([unit(t, hh) for t in range(seq // tq) for hh in range(2)], s_ref, o_ref)


def _band_attn(q, k, v, w):
    batch, seq, _ = v.shape
    pair_spec = pl.BlockSpec((1, seq, LANES), lambda p, b: (b, 0, p))
    return pl.pallas_call(
        _band_attn_kernel,
        grid=(HEADS // 2, batch),
        in_specs=[pair_spec, pair_spec, pair_spec,
                  pl.BlockSpec((2, 1, BAND_PERIOD), lambda p, b: (p, 0, 0))],
        out_specs=pair_spec,
        out_shape=jax.ShapeDtypeStruct((batch, seq, BRANCH_WIDTH), BF16),
        scratch_shapes=[pltpu.VMEM((2, ATTN_TQ, BAND_WIDTH), F32), pltpu.VMEM((2, ATTN_TQ, BAND_WIDTH), F32)],
        compiler_params=pltpu.CompilerParams(dimension_semantics=("parallel", "arbitrary"),
                                             vmem_limit_bytes=VMEM_LIMIT),
        name="band_attn",
    )(q, k, v, w)


def _merge_kernel(x_ref, ya_ref, yb_ref, yc_ref, gmix_ref, wg_ref, bg_ref, wb_ref, wo_ref, o_ref):
    x = x_ref[...]
    h = _row_rms(x, gmix_ref[...]).astype(BF16)
    merged = None
    for n, y_ref in enumerate((ya_ref, yb_ref, yc_ref)):
        cols = slice(n * D_MODEL, (n + 1) * D_MODEL)
        gate = jax.nn.sigmoid(_dot(h, wg_ref[:, cols]) + bg_ref[:, cols])
        term = gate * _dot(y_ref[...], wb_ref[n])
        merged = term if merged is None else merged + term
    o_ref[...] = x + _dot(merged.astype(BF16), wo_ref[...])


def _merge(x, ya, yb, yc, gmix, wg, bg, wb, wo):
    t = x.shape[0]
    tm = TOKEN_TILE
    row = lambda width: pl.BlockSpec((tm, width), lambda i: (i, 0))
    consts = [gmix, wg, bg, wb, wo]
    return pl.pallas_call(
        _merge_kernel,
        grid=(t // tm,),
        in_specs=[row(D_MODEL)] + [row(BRANCH_WIDTH)] * 3 + [_const_spec(c) for c in consts],
        out_specs=row(D_MODEL),
        out_shape=jax.ShapeDtypeStruct((t, D_MODEL), F32),
        compiler_params=_params(1),
        name="merge",
    )(x, ya, yb, yc, *_arrays(consts))


FF_CHUNK = 1024


def _ffn_kernel(x_ref, g_ref, w1_ref, w2_ref, o_ref):
    x = x_ref[...]
    h = _row_rms(x, g_ref[...]).astype(BF16)
    acc = x
    for c in range(D_FF // FF_CHUNK):
        a = jnp.maximum(_dot(h, w1_ref[:, c * FF_CHUNK:(c + 1) * FF_CHUNK]), 0.0)
        acc = acc + _dot((a * a).astype(BF16), w2_ref[c * FF_CHUNK:(c + 1) * FF_CHUNK, :])
    o_ref[...] = acc


def _ffn(x, g, w1, w2):
    t = x.shape[0]
    tm = TOKEN_TILE
    row = pl.BlockSpec((tm, D_MODEL), lambda i: (i, 0))
    return pl.pallas_call(
        _ffn_kernel,
        grid=(t // tm,),
        in_specs=[row] + [_const_spec(c) for c in (g, w1, w2)],
        out_specs=row,
        out_shape=jax.ShapeDtypeStruct((t, D_MODEL), F32),
        compiler_params=_params(1),
        name="ffn",
    )(x, *_arrays((g, w1, w2)))


def _group_matrix(groups):
    ids = np.asarray(groups)
    return jnp.asarray((ids[:, None] == ids[None, :]) & (ids[:, None] >= 0), dtype=BF16)


def _band_table(rel_bias):
    rb = rel_bias.astype(F32) * LOG2E
    left = BAND_WIDTH - ATTN_TQ - REL_CLIP
    far = jnp.broadcast_to(rb[..., 2 * REL_CLIP:], rb.shape[:-1] + (BAND_PERIOD,))
    w = jnp.concatenate([far[..., :left], rb[..., ::-1], far[..., left + 2 * REL_CLIP + 1:]], axis=-1)
    return w[..., None, :]


def _rope_tables(seq):
    half = B_ROPE // 2
    inv = ROPE_THETA ** (-jnp.arange(half, dtype=F32) / half)
    ang = jnp.arange(seq).astype(F32)[:, None] * inv[None, :]
    cos, sin = jnp.cos(ang), jnp.sin(ang)
    pad_l = jnp.ones((seq, MISC_KR), F32)
    pad_r = jnp.ones((seq, SLOT - MISC_KR - B_ROPE), F32)
    cos_t = jnp.concatenate([pad_l, cos, cos, pad_r], axis=1)
    sin_t = jnp.concatenate([0 * pad_l, -sin, sin, 0 * pad_r], axis=1)
    return cos_t, sin_t


def kernel(x, norm_mix, w_in, b_forget, b_gate, qk_norm_a, mla_q_norm, mla_kv_norm, w_q_up, w_kv_up,
           qk_norm_b_nope, qk_norm_b_rope, qk_norm_c, rel_bias, w_branch, w_out, norm_ffn, w_ff1, w_ff2):
    batch, seq, _ = x.shape
    depth = w_in.shape[0]
    o = IN_OFFS
    zeros = lambda *s: jnp.zeros(s, F32)

    misc_w = jnp.concatenate([w_in[:, :, o[3]:o[4]], zeros(depth, D_MODEL, MISC_KR - HEADS),
                              w_in[:, :, o[6]:o[7]], zeros(depth, D_MODEL, SLOT - MISC_KR - B_ROPE)], axis=2)
    w_pack = jnp.concatenate([w_in[:, :, o[0]:o[3]], w_in[:, :, o[7]:o[10]], w_in[:, :, o[4]:o[6]], misc_w],
                             axis=2).astype(BF16)
    w_gate = w_in[:, :, o[10]:o[11]].astype(BF16)

    attn_scale = HEAD_DIM ** -0.5 * LOG2E
    tile8 = lambda g: jnp.tile(g, (1, HEADS))
    hgain = jnp.stack([tile8(qk_norm_a[:, 0]) * attn_scale, tile8(qk_norm_a[:, 1]),
                       tile8(qk_norm_c[:, 0]) * attn_scale, tile8(qk_norm_c[:, 1])], axis=1)

    mla_scale = (B_NOPE + B_ROPE) ** -0.5 * LOG2E
    pad = SLOT - B_NOPE - B_ROPE
    wq_slot = jnp.pad(w_q_up.reshape(depth, B_Q_LORA, HEADS, B_NOPE + B_ROPE),
                      ((0, 0), (0, 0), (0, 0), (0, pad))).reshape(depth, B_Q_LORA, HEADS * SLOT).astype(BF16)
    wkv = w_kv_up.reshape(depth, B_KV_LORA, HEADS, 2 * B_NOPE)
    wk_slot = jnp.pad(wkv[..., :B_NOPE], ((0, 0), (0, 0), (0, 0), (0, SLOT - B_NOPE))
                      ).reshape(depth, B_KV_LORA, HEADS * SLOT).astype(BF16)
    wv = wkv[..., B_NOPE:].reshape(depth, B_KV_LORA, BRANCH_WIDTH).astype(BF16)
    gq_slot = tile8(jnp.concatenate([qk_norm_b_nope[:, 0], qk_norm_b_rope[:, 0], zeros(depth, pad)], axis=1)
                    ) * mla_scale
    gk_slot = tile8(jnp.concatenate([qk_norm_b_nope[:, 1], zeros(depth, SLOT - B_NOPE)], axis=1))
    gkr_slot = jnp.concatenate([zeros(depth, MISC_KR), qk_norm_b_rope[:, 1], zeros(depth, pad)], axis=1)
    slot_groups = np.concatenate([np.zeros(B_NOPE), np.ones(B_ROPE), -np.ones(pad)]).astype(np.int64)
    slot_groups2 = np.concatenate([slot_groups, np.where(slot_groups >= 0, slot_groups + 2, -1)])
    gslot = _group_matrix(slot_groups2)
    invcnt = jnp.asarray(np.tile(np.concatenate([np.full(B_NOPE, 1.0 / B_NOPE), np.full(B_ROPE, 1.0 / B_ROPE),
                                                  np.ones(pad)]), 2)[None, :], dtype=F32)
    tri = jnp.asarray(np.tril(np.ones((MXU_DIM, MXU_DIM))), dtype=BF16)
    cos_t, sin_t = _rope_tables(seq)
    bf_rows = jnp.pad(b_forget.astype(F32), ((0, 0), (0, LANES - HEADS)))
    band_tables = _band_table(rel_bias)

    w_branch_b = w_branch.astype(BF16)
    w_out_b = w_out.astype(BF16)
    w_ff1_b = w_ff1.astype(BF16)
    w_ff2_b = w_ff2.astype(BF16)

    xt = x.reshape(batch * seq, D_MODEL)
    seq3 = lambda a: a.reshape(batch, seq, a.shape[-1])
    for l in range(depth):
        qa, ka, va, qc, kc, vc, qdn, kvdn, misc = _in_proj(
            xt, norm_mix[l][None], _Layer(w_pack, l), _Layer(hgain, l), mla_q_norm[l][None],
            mla_kv_norm[l][None])
        aq, ak = _fox_prep(misc, bf_rows[l][None], tri, batch, seq)
        qb, kb, vb = _mla_prep(qdn, kvdn, misc, cos_t, sin_t, _Layer(wq_slot, l), _Layer(wk_slot, l),
                               _Layer(wv, l), gslot, invcnt, gq_slot[l][None], gk_slot[l][None], gkr_slot[l][None], seq)
        ya = _prefix_attn(seq3(qa), seq3(ka), seq3(va), aq, ak, frame_causal=True, name="fox_attn")
        yb = _prefix_attn(seq3(qb), seq3(kb), seq3(vb), frame_causal=False, name="mla_attn")
        yc = _band_attn(seq3(qc), seq3(kc), seq3(vc), band_tables[l])
        flat = lambda a: a.reshape(batch * seq, BRANCH_WIDTH)
        xt = _merge(xt, flat(ya), flat(yb), flat(yc), norm_mix[l][None], _Layer(w_gate, l), b_gate[l][None],
                    _Layer(w_branch_b, l), _Layer(w_out_b, l))
        xt = _ffn(xt, norm_ffn[l][None], _Layer(w_ff1_b, l), _Layer(w_ff2_b, l))
    return xt.reshape(batch, seq, D_MODEL)
```

```python
import functools
from typing import NamedTuple

import numpy as np
import jax
import jax.numpy as jnp
from jax import lax
from jax.experimental import pallas as pl
from jax.experimental.pallas import tpu as pltpu

F32 = jnp.float32
BF16 = jnp.bfloat16

D_MODEL = 1024
DEPTH = 4
CHUNK = 64
EPS = 1e-6
NEG_INF = -1e30
HEADS = 8
HEAD_DIM = 64
B_NOPE = 64
B_ROPE = 32
B_Q_LORA = 384
B_KV_LORA = 256
ROPE_THETA = 10000.0
C_LEFT_CHUNKS = 8
REL_CLIP = 256
N_BRANCH = 3
BRANCH_WIDTH = 512
D_FF = 4 * D_MODEL
IN_SIZES = (512, 512, 512, HEADS, B_Q_LORA, B_KV_LORA, B_ROPE, 512, 512, 512, N_BRANCH * D_MODEL)
IN_OFFS = tuple(int(v) for v in np.cumsum((0,) + IN_SIZES))

LANES = 128
MXU_DIM = 256
SLOT = LANES
TOKEN_TILE = 1024
ATTN_TQ = 256
PAIRS_PER_STEP = 2
VMEM_LIMIT = 56 * 1024 * 1024

PACK_QD = 6 * 512
PACK_KVD = PACK_QD + B_Q_LORA
PACK_MISC = PACK_KVD + B_KV_LORA
PACK_WIDTH = PACK_MISC + LANES
MISC_KR = 64
LOG2E = float(np.log2(np.e))


def _dot(a, b):
    return jnp.dot(a, b, preferred_element_type=F32)


def _dot_nt(a, b):
    return lax.dot_general(a, b, (((1,), (1,)), ((), ())), preferred_element_type=F32)


def _split2(v):
    hi = v.astype(BF16)
    lo = (v - hi.astype(F32)).astype(BF16)
    return hi, lo


def _group_sumsq(u, g_ref):
    hi, lo = _split2(u * u)
    g = g_ref[...]
    return _dot(hi, g) + _dot(lo, g)


def _row_rms(x, gain):
    ms = jnp.mean(x * x, axis=-1, keepdims=True)
    return x * lax.rsqrt(ms + EPS) * gain


class _Layer(NamedTuple):
    stacked: jax.Array
    index: int


def _const_spec(op):
    if isinstance(op, _Layer):
        tail = op.stacked.shape[1:]
        return pl.BlockSpec((None,) + tail, lambda *_: (op.index,) + (0,) * len(tail),
                            pipeline_mode=pl.Buffered(1))
    return pl.BlockSpec(op.shape, lambda *_: (0,) * op.ndim, pipeline_mode=pl.Buffered(1))


def _arrays(ops):
    return [op.stacked if isinstance(op, _Layer) else op for op in ops]


def _params(n_axes):
    return pltpu.CompilerParams(dimension_semantics=("parallel",) * n_axes,
                                vmem_limit_bytes=VMEM_LIMIT)


def _in_proj_kernel(x_ref, gmix_ref, w_ref, hgain_ref, gq_ref, gkv_ref,
                    qa_ref, ka_ref, va_ref, qc_ref, kc_ref, vc_ref, qd_ref, kvd_ref, misc_ref):
    h = _row_rms(x_ref[...], gmix_ref[...]).astype(BF16)

    first_head = lax.broadcasted_iota(jnp.int32, (1, LANES), 1) < HEAD_DIM

    def head_normed(col, gain_row):
        u = _dot(h, w_ref[:, col:col + 512])
        parts = []
        for j in range(512 // LANES):
            ub = u[:, LANES * j:LANES * (j + 1)]
            sq = ub * ub
            s0 = jnp.sum(jnp.where(first_head, sq, 0.0), axis=-1, keepdims=True)
            s1 = jnp.sum(jnp.where(first_head, 0.0, sq), axis=-1, keepdims=True)
            parts.append(ub * lax.rsqrt(jnp.where(first_head, s0, s1) * (1.0 / HEAD_DIM) + EPS))
        return (jnp.concatenate(parts, axis=1) * hgain_ref[gain_row:gain_row + 1, :]).astype(BF16)

    qa_ref[...] = head_normed(0, 0)
    ka_ref[...] = head_normed(512, 1)
    va_ref[...] = _dot(h, w_ref[:, 1024:1536]).astype(BF16)
    qc_ref[...] = head_normed(1536, 2)
    kc_ref[...] = head_normed(2048, 3)
    vc_ref[...] = _dot(h, w_ref[:, 2560:3072]).astype(BF16)
    qd_ref[...] = _row_rms(_dot(h, w_ref[:, PACK_QD:PACK_KVD]), gq_ref[...]).astype(BF16)
    kvd_ref[...] = _row_rms(_dot(h, w_ref[:, PACK_KVD:PACK_MISC]), gkv_ref[...]).astype(BF16)
    misc_ref[...] = _dot(h, w_ref[:, PACK_MISC:PACK_WIDTH])


def _in_proj(x, gmix, w_pack, hgain, gq, gkv):
    t = x.shape[0]
    tm = TOKEN_TILE
    row = lambda width: pl.BlockSpec((tm, width), lambda i: (i, 0))
    out_shapes = [jax.ShapeDtypeStruct((t, 512), BF16)] * 6 + [
        jax.ShapeDtypeStruct((t, B_Q_LORA), BF16),
        jax.ShapeDtypeStruct((t, B_KV_LORA), BF16),
        jax.ShapeDtypeStruct((t, LANES), F32)]
    consts = [gmix, w_pack, hgain, gq, gkv]
    return pl.pallas_call(
        _in_proj_kernel,
        grid=(t // tm,),
        in_specs=[row(D_MODEL)] + [_const_spec(c) for c in consts],
        out_specs=[row(512)] * 6 + [row(B_Q_LORA), row(B_KV_LORA), row(LANES)],
        out_shape=out_shapes,
        compiler_params=_params(1),
        name="in_proj",
    )(x, *_arrays(consts))


def _split3(v):
    hi = v.astype(BF16)
    r1 = v - hi.astype(F32)
    mid = r1.astype(BF16)
    lo = (r1 - mid.astype(F32)).astype(BF16)
    return hi, mid, lo


def _fox_prep_kernel(misc_ref, bf_ref, tri_ref, pq_ref, pk_ref, oq_ref, ok_ref, aq_ref, ak_ref):
    blk = MXU_DIM
    seq = misc_ref.shape[0]
    lane = lax.broadcasted_iota(jnp.int32, (blk, LANES), 1)
    tri = tri_ref[...]
    carry = jnp.zeros((1, LANES), F32)
    for j in range(seq // blk):
        rows = slice(j * blk, (j + 1) * blk)
        z = misc_ref[rows, :] + bf_ref[...]
        lf = -(jnp.maximum(-z, 0.0) + jnp.log1p(jnp.exp(-jnp.abs(z))))
        lf = jnp.where(lane < HEADS, lf, 0.0)
        hi, mid, lo = _split3(lf)
        c = _dot(tri, hi) + _dot(tri, mid) + _dot(tri, lo) + carry
        carry = c[blk - 1:blk, :]
        c3 = jnp.concatenate(_split3(c * LOG2E), axis=1)
        aq_ref[0, rows, :] = (_dot(c3, pq_ref[...]) + oq_ref[...]).astype(BF16)
        ak_ref[0, rows, :] = (_dot(c3, pk_ref[...]) + ok_ref[...]).astype(BF16)


AUG_LANES = LANES // HEADS


def _fox_prep(misc, bf_row, tri, batch, seq):
    pq = np.zeros((3 * LANES, LANES), np.float32)
    pk = np.zeros((3 * LANES, LANES), np.float32)
    oq = np.zeros((1, LANES), np.float32)
    ok = np.zeros((1, LANES), np.float32)
    for h in range(HEADS):
        for t in range(3):
            pq[t * LANES + h, h * AUG_LANES + t] = 1.0
            pk[t * LANES + h, h * AUG_LANES + 3 + t] = -1.0
            oq[0, h * AUG_LANES + 3 + t] = 1.0
            ok[0, h * AUG_LANES + t] = 1.0
    consts = [bf_row, tri, jnp.asarray(pq, BF16), jnp.asarray(pk, BF16), jnp.asarray(oq), jnp.asarray(ok)]
    aug_spec = pl.BlockSpec((1, seq, LANES), lambda b: (b, 0, 0))
    aug_shape = jax.ShapeDtypeStruct((batch, seq, LANES), BF16)
    return pl.pallas_call(
        _fox_prep_kernel,
        grid=(batch,),
        in_specs=[pl.BlockSpec((seq, LANES), lambda b: (b, 0))] + [_const_spec(c) for c in consts],
        out_specs=[aug_spec, aug_spec],
        out_shape=[aug_shape, aug_shape],
        compiler_params=_params(1),
        name="fox_prep",
    )(misc, *consts)


def _rope_slot(x, cos, sin, lo_half):
    partner = jnp.where(lo_half, pltpu.roll(x, LANES - B_ROPE // 2, 1), pltpu.roll(x, B_ROPE // 2, 1))
    return x * cos + partner * sin


def _mla_prep_kernel(qd_ref, kvd_ref, misc_ref, cos_ref, sin_ref, wq_ref, wk_ref, wv_ref,
                     gslot_ref, invcnt_ref, gq_ref, gk_ref, gkr_ref, q_ref, k_ref, v_ref):
    tm = qd_ref.shape[0]
    lane = lax.broadcasted_iota(jnp.int32, (tm, SLOT), 1)
    lo_half = lane < MISC_KR + B_ROPE // 2
    in_kr = (lane >= MISC_KR) & (lane < MISC_KR + B_ROPE)
    cos = cos_ref[...]
    sin = sin_ref[...]
    qd = qd_ref[...]
    kvd = kvd_ref[...]

    def slot_normed(u, gain_ref, col):
        ss = _group_sumsq(u, gslot_ref)
        return u * lax.rsqrt(ss * invcnt_ref[...] + EPS) * gain_ref[:, col:col + MXU_DIM]

    m = misc_ref[...]
    ss_kr = jnp.sum(jnp.where(in_kr, m * m, 0.0), axis=-1, keepdims=True)
    kr = _rope_slot(m * lax.rsqrt(ss_kr * (1.0 / B_ROPE) + EPS) * gkr_ref[...], cos, sin, lo_half)

    for j in range(HEADS * SLOT // MXU_DIM):
        col = j * MXU_DIM
        qn = slot_normed(_dot(qd, wq_ref[:, col:col + MXU_DIM]), gq_ref, col)
        kn = slot_normed(_dot(kvd, wk_ref[:, col:col + MXU_DIM]), gk_ref, col)
        for s in range(MXU_DIM // SLOT):
            sl = slice(s * SLOT, (s + 1) * SLOT)
            q_ref[:, col + s * SLOT:col + (s + 1) * SLOT] = _rope_slot(qn[:, sl], cos, sin, lo_half).astype(BF16)
            k_ref[:, col + s * SLOT:col + (s + 1) * SLOT] = (kn[:, sl] + kr).astype(BF16)
    v_ref[...] = _dot(kvd, wv_ref[...]).astype(BF16)


def _mla_prep(qdn, kvdn, misc, cos_t, sin_t, wq, wk, wv, gslot, invcnt, gq, gk, gkr, seq):
    t = qdn.shape[0]
    tm = TOKEN_TILE
    per_seq = seq // tm
    row = lambda width: pl.BlockSpec((tm, width), lambda i: (i, 0))
    pos = pl.BlockSpec((tm, SLOT), lambda i: (i % per_seq, 0))
    consts = [wq, wk, wv, gslot, invcnt, gq, gk, gkr]
    return pl.pallas_call(
        _mla_prep_kernel,
        grid=(t // tm,),
        in_specs=[row(B_Q_LORA), row(B_KV_LORA), row(LANES), pos, pos] + [_const_spec(c) for c in consts],
        out_specs=[row(HEADS * SLOT), row(HEADS * SLOT), row(BRANCH_WIDTH)],
        out_shape=[jax.ShapeDtypeStruct((t, HEADS * SLOT), BF16),
                   jax.ShapeDtypeStruct((t, HEADS * SLOT), BF16),
                   jax.ShapeDtypeStruct((t, BRANCH_WIDTH), BF16)],
        compiler_params=_params(1),
        name="mla_prep",
    )(qdn, kvdn, misc, cos_t, sin_t, *_arrays(consts))


def _head_lanes(lane, hh):
    return (lane >= hh * HEAD_DIM) & (lane < (hh + 1) * HEAD_DIM)


def _spread_order(n):
    return list(range(0, n, 2)) + list(range(n - 1 - (n % 2), 0, -2))


def _run_units(units, s_ref, o_ref):
    depth = s_ref.shape[0]
    lane = lax.broadcasted_iota(jnp.int32, (1, LANES), 1)
    stats = {}

    def scores(n):
        pieces = units[n][0]()
        for c0, blk in pieces:
            s_ref[n % depth, :, c0:c0 + blk.shape[1]] = blk
        stats[n] = (functools.reduce(jnp.maximum, [jnp.max(blk, axis=-1, keepdims=True) for _, blk in pieces]),
                    sum(blk.shape[1] for _, blk in pieces))

    def finish(n):
        _, values, rows, head = units[n]
        m, klen = stats.pop(n)
        p = jnp.exp2(s_ref[n % depth, :, 0:klen] - m)
        l = jnp.sum(p, axis=-1, keepdims=True)
        o = (_dot(p.astype(BF16), values(0, klen)) / l).astype(BF16)
        cols = slice(head // 2 * LANES, (head // 2 + 1) * LANES)
        if head % 2 == 0:
            o_ref[0, rows, cols] = o
        else:
            o_ref[0, rows, cols] = jnp.where(lane < HEAD_DIM, o_ref[0, rows, cols], o)

    for n in range(min(depth, len(units))):
        scores(n)
    for n in range(len(units)):
        finish(n)
        if n + depth < len(units):
            scores(n + depth)


def _prefix_attn_kernel(*refs, frame_causal, decay):
    heads = 2 * PAIRS_PER_STEP
    if decay:
        q_ref, k_ref, v_ref, aq_ref, ak_ref, o_ref, s_ref, kx_ref = refs
        for pp in range(PAIRS_PER_STEP):
            kx_ref[pp, :, 0:LANES] = k_ref[0, :, pp * LANES:(pp + 1) * LANES]
            kx_ref[pp, :, LANES:2 * LANES] = ak_ref[0]
    else:
        q_ref, k_ref, v_ref, o_ref, s_ref = refs
    seq = q_ref.shape[1]
    tq = ATTN_TQ
    lane = lax.broadcasted_iota(jnp.int32, (1, LANES), 1)
    row = lax.broadcasted_iota(jnp.int32, (tq, tq), 0)
    col = lax.broadcasted_iota(jnp.int32, (tq, tq), 1)
    if frame_causal:
        diag_ok = col <= row
    else:
        diag_ok = (col // CHUNK) <= (row // CHUNK)

    def unit(i, head):
        q0, q1 = i * tq, (i + 1) * tq
        pp, hh = divmod(head, 2)

        def score_blocks():
            if decay:
                q2 = q_ref[0, q0:q1, pp * LANES:(pp + 1) * LANES]
                a2 = aq_ref[0, q0:q1, :]
                aug0 = (heads * pl.program_id(1) + head) * AUG_LANES
                qh = jnp.concatenate(
                    [jnp.where(_head_lanes(lane, hh), q2, jnp.zeros_like(q2)),
                     jnp.where((lane >= aug0) & (lane < aug0 + AUG_LANES), a2, jnp.zeros_like(a2))], axis=1)
                k_of = lambda a, b: kx_ref[pp, a:b, :]
            else:
                qh = q_ref[0, q0:q1, head * SLOT:(head + 1) * SLOT]
                k_of = lambda a, b: k_ref[0, a:b, head * SLOT:(head + 1) * SLOT]
            pieces = [(0, _dot_nt(qh, k_of(0, q0)))] if i else []
            return pieces + [(q0, jnp.where(diag_ok, _dot_nt(qh, k_of(q0, q1)), NEG_INF))]

        return (score_blocks, lambda a, b: v_ref[0, a:b, pp * LANES:(pp + 1) * LANES], slice(q0, q1), head)

    _run_units([unit(i, head) for i in _spread_order(seq // tq) for head in range(heads)], s_ref, o_ref)


def _prefix_attn(q, k, v, aq=None, ak=None, *, frame_causal, name):
    batch, seq, _ = v.shape
    decay = aq is not None
    vw = PAIRS_PER_STEP * LANES
    qk_w = vw if decay else 2 * PAIRS_PER_STEP * SLOT
    group_spec = lambda w: pl.BlockSpec((1, seq, w), lambda b, g: (b, 0, g))
    in_specs = [group_spec(qk_w), group_spec(qk_w), group_spec(vw)]
    args = [q, k, v]
    scratch = [pltpu.VMEM((2, ATTN_TQ, seq), F32)]
    if decay:
        aug_spec = pl.BlockSpec((1, seq, LANES), lambda b, g: (b, 0, 0))
        in_specs += [aug_spec, aug_spec]
        args += [aq, ak]
        scratch += [pltpu.VMEM((PAIRS_PER_STEP, seq, 2 * LANES), BF16)]
    return pl.pallas_call(
        functools.partial(_prefix_attn_kernel, frame_causal=frame_causal, decay=decay),
        grid=(batch, HEADS // (2 * PAIRS_PER_STEP)),
        in_specs=in_specs,
        out_specs=group_spec(vw),
        out_shape=jax.ShapeDtypeStruct((batch, seq, BRANCH_WIDTH), BF16),
        scratch_shapes=scratch,
        compiler_params=_params(2),
        name=name,
    )(*args)


BAND_BLOCKS = C_LEFT_CHUNKS * CHUNK // ATTN_TQ + 1
BAND_WIDTH = BAND_BLOCKS * ATTN_TQ
BAND_PERIOD = BAND_WIDTH + ATTN_TQ


def _band_attn_kernel(q_ref, k_ref, v_ref, w_ref, o_ref, bias_ref, s_ref):
    heads = 2 * PAIRS_PER_STEP
    seq = q_ref.shape[1]
    tq = ATTN_TQ
    lane = lax.broadcasted_iota(jnp.int32, (1, LANES), 1)

    @pl.when(pl.program_id(1) == 0)
    def _():
        r = lax.broadcasted_iota(jnp.int32, (tq, BAND_WIDTH), 0)
        c = lax.broadcasted_iota(jnp.int32, (tq, BAND_WIDTH), 1)
        dchunk = r // CHUNK - (c // CHUNK - C_LEFT_CHUNKS)
        valid = (dchunk >= 0) & (dchunk <= C_LEFT_CHUNKS)
        for head in range(heads):
            table = jnp.broadcast_to(w_ref[head], (tq, BAND_PERIOD))
            toeplitz = pltpu.roll(table, 0, 1, stride=1, stride_axis=0)
            bias_ref[head] = jnp.where(valid, toeplitz[:, :BAND_WIDTH], NEG_INF)

    def unit(t, head):
        k0, k1 = max(0, t - (BAND_BLOCKS - 1)) * tq, (t + 1) * tq
        boff = BAND_WIDTH - (k1 - k0)
        pp, hh = divmod(head, 2)
        cols = slice(pp * LANES, (pp + 1) * LANES)

        def score_blocks():
            q2 = q_ref[0, t * tq:(t + 1) * tq, cols]
            qh = jnp.where(_head_lanes(lane, hh), q2, jnp.zeros_like(q2))
            return [(0, _dot_nt(qh, k_ref[0, k0:k1, cols]) + bias_ref[head, :, boff:])]

        return (score_blocks, lambda a, b: v_ref[0, k0 + a:k0 + b, cols], slice(t * tq, (t + 1) * tq), head)

    _run_units([unit(t, head) for t in range(seq // tq) for head in range(heads)], s_ref, o_ref)


def _band_attn(q, k, v, w):
    batch, seq, _ = v.shape
    heads = 2 * PAIRS_PER_STEP
    group_spec = pl.BlockSpec((1, seq, PAIRS_PER_STEP * LANES), lambda g, b: (b, 0, g))
    return pl.pallas_call(
        _band_attn_kernel,
        grid=(HEADS // heads, batch),
        in_specs=[group_spec, group_spec, group_spec,
                  pl.BlockSpec((heads, 1, BAND_PERIOD), lambda g, b: (g, 0, 0))],
        out_specs=group_spec,
        out_shape=jax.ShapeDtypeStruct((batch, seq, BRANCH_WIDTH), BF16),
        scratch_shapes=[pltpu.VMEM((heads, ATTN_TQ, BAND_WIDTH), F32), pltpu.VMEM((2, ATTN_TQ, BAND_WIDTH), F32)],
        compiler_params=pltpu.CompilerParams(dimension_semantics=("parallel", "arbitrary"),
                                             vmem_limit_bytes=VMEM_LIMIT),
        name="band_attn",
    )(q, k, v, w)


def _merge_kernel(x_ref, ya_ref, yb_ref, yc_ref, gmix_ref, wg_ref, bg_ref, wb_ref, wo_ref, o_ref):
    x = x_ref[...]
    h = _row_rms(x, gmix_ref[...]).astype(BF16)
    merged = None
    for n, y_ref in enumerate((ya_ref, yb_ref, yc_ref)):
        cols = slice(n * D_MODEL, (n + 1) * D_MODEL)
        gate = jax.nn.sigmoid(_dot(h, wg_ref[:, cols]) + bg_ref[:, cols])
        term = gate * _dot(y_ref[...], wb_ref[n])
        merged = term if merged is None else merged + term
    o_ref[...] = x + _dot(merged.astype(BF16), wo_ref[...])


def _merge(x, ya, yb, yc, gmix, wg, bg, wb, wo):
    t = x.shape[0]
    tm = TOKEN_TILE
    row = lambda width: pl.BlockSpec((tm, width), lambda i: (i, 0))
    consts = [gmix, wg, bg, wb, wo]
    return pl.pallas_call(
        _merge_kernel,
        grid=(t // tm,),
        in_specs=[row(D_MODEL)] + [row(BRANCH_WIDTH)] * 3 + [_const_spec(c) for c in consts],
        out_specs=row(D_MODEL),
        out_shape=jax.ShapeDtypeStruct((t, D_MODEL), F32),
        compiler_params=_params(1),
        name="merge",
    )(x, ya, yb, yc, *_arrays(consts))


FF_CHUNK = 1024


def _ffn_kernel(x_ref, g_ref, w1_ref, w2_ref, o_ref):
    x = x_ref[...]
    h = _row_rms(x, g_ref[...]).astype(BF16)
    acc = x
    for c in range(D_FF // FF_CHUNK):
        a = jnp.maximum(_dot(h, w1_ref[:, c * FF_CHUNK:(c + 1) * FF_CHUNK]), 0.0)
        acc = acc + _dot((a * a).astype(BF16), w2_ref[c * FF_CHUNK:(c + 1) * FF_CHUNK, :])
    o_ref[...] = acc


def _ffn(x, g, w1, w2):
    t = x.shape[0]
    tm = TOKEN_TILE
    row = pl.BlockSpec((tm, D_MODEL), lambda i: (i, 0))
    return pl.pallas_call(
        _ffn_kernel,
        grid=(t // tm,),
        in_specs=[row] + [_const_spec(c) for c in (g, w1, w2)],
        out_specs=row,
        out_shape=jax.ShapeDtypeStruct((t, D_MODEL), F32),
        compiler_params=_params(1),
        name="ffn",
    )(x, *_arrays((g, w1, w2)))


def _group_matrix(groups):
    ids = np.asarray(groups)
    return jnp.asarray((ids[:, None] == ids[None, :]) & (ids[:, None] >= 0), dtype=BF16)


def _band_table(rel_bias):
    rb = rel_bias.astype(F32) * LOG2E
    left = BAND_WIDTH - ATTN_TQ - REL_CLIP
    far = jnp.broadcast_to(rb[..., 2 * REL_CLIP:], rb.shape[:-1] + (BAND_PERIOD,))
    w = jnp.concatenate([far[..., :left], rb[..., ::-1], far[..., left + 2 * REL_CLIP + 1:]], axis=-1)
    return w[..., None, :]


def _rope_tables(seq):
    half = B_ROPE // 2
    inv = ROPE_THETA ** (-jnp.arange(half, dtype=F32) / half)
    ang = jnp.arange(seq).astype(F32)[:, None] * inv[None, :]
    cos, sin = jnp.cos(ang), jnp.sin(ang)
    pad_l = jnp.ones((seq, MISC_KR), F32)
    pad_r = jnp.ones((seq, SLOT - MISC_KR - B_ROPE), F32)
    cos_t = jnp.concatenate([pad_l, cos, cos, pad_r], axis=1)
    sin_t = jnp.concatenate([0 * pad_l, -sin, sin, 0 * pad_r], axis=1)
    return cos_t, sin_t


def kernel(x, norm_mix, w_in, b_forget, b_gate, qk_norm_a, mla_q_norm, mla_kv_norm, w_q_up, w_kv_up,
           qk_norm_b_nope, qk_norm_b_rope, qk_norm_c, rel_bias, w_branch, w_out, norm_ffn, w_ff1, w_ff2):
    batch, seq, _ = x.shape
    depth = w_in.shape[0]
    o = IN_OFFS
    zeros = lambda *s: jnp.zeros(s, F32)

    misc_w = jnp.concatenate([w_in[:, :, o[3]:o[4]], zeros(depth, D_MODEL, MISC_KR - HEADS),
                              w_in[:, :, o[6]:o[7]], zeros(depth, D_MODEL, SLOT - MISC_KR - B_ROPE)], axis=2)
    w_pack = jnp.concatenate([w_in[:, :, o[0]:o[3]], w_in[:, :, o[7]:o[10]], w_in[:, :, o[4]:o[6]], misc_w],
                             axis=2).astype(BF16)
    w_gate = w_in[:, :, o[10]:o[11]].astype(BF16)

    attn_scale = HEAD_DIM ** -0.5 * LOG2E
    tile8 = lambda g: jnp.tile(g, (1, HEADS))
    hgain = jnp.stack([tile8(qk_norm_a[:, 0]) * attn_scale, tile8(qk_norm_a[:, 1]),
                       tile8(qk_norm_c[:, 0]) * attn_scale, tile8(qk_norm_c[:, 1])], axis=1)

    mla_scale = (B_NOPE + B_ROPE) ** -0.5 * LOG2E
    pad = SLOT - B_NOPE - B_ROPE
    wq_slot = jnp.pad(w_q_up.reshape(depth, B_Q_LORA, HEADS, B_NOPE + B_ROPE),
                      ((0, 0), (0, 0), (0, 0), (0, pad))).reshape(depth, B_Q_LORA, HEADS * SLOT).astype(BF16)
    wkv = w_kv_up.reshape(depth, B_KV_LORA, HEADS, 2 * B_NOPE)
    wk_slot = jnp.pad(wkv[..., :B_NOPE], ((0, 0), (0, 0), (0, 0), (0, SLOT - B_NOPE))
                      ).reshape(depth, B_KV_LORA, HEADS * SLOT).astype(BF16)
    wv = wkv[..., B_NOPE:].reshape(depth, B_KV_LORA, BRANCH_WIDTH).astype(BF16)
    gq_slot = tile8(jnp.concatenate([qk_norm_b_nope[:, 0], qk_norm_b_rope[:, 0], zeros(depth, pad)], axis=1)
                    ) * mla_scale
    gk_slot = tile8(jnp.concatenate([qk_norm_b_nope[:, 1], zeros(depth, SLOT - B_NOPE)], axis=1))
    gkr_slot = jnp.concatenate([zeros(depth, MISC_KR), qk_norm_b_rope[:, 1], zeros(depth, pad)], axis=1)
    slot_groups = np.concatenate([np.zeros(B_NOPE), np.ones(B_ROPE), -np.ones(pad)]).astype(np.int64)
    slot_groups2 = np.concatenate([slot_groups, np.where(slot_groups >= 0, slot_groups + 2, -1)])
    gslot = _group_matrix(slot_groups2)
    invcnt = jnp.asarray(np.tile(np.concatenate([np.full(B_NOPE, 1.0 / B_NOPE), np.full(B_ROPE, 1.0 / B_ROPE),
                                                  np.ones(pad)]), 2)[None, :], dtype=F32)
    tri = jnp.asarray(np.tril(np.ones((MXU_DIM, MXU_DIM))), dtype=BF16)
    cos_t, sin_t = _rope_tables(seq)
    bf_rows = jnp.pad(b_forget.astype(F32), ((0, 0), (0, LANES - HEADS)))
    band_tables = _band_table(rel_bias)

    w_branch_b = w_branch.astype(BF16)
    w_out_b = w_out.astype(BF16)
    w_ff1_b = w_ff1.astype(BF16)
    w_ff2_b = w_ff2.astype(BF16)

    xt = x.reshape(batch * seq, D_MODEL)
    seq3 = lambda a: a.reshape(batch, seq, a.shape[-1])
    for l in range(depth):
        qa, ka, va, qc, kc, vc, qdn, kvdn, misc = _in_proj(
            xt, norm_mix[l][None], _Layer(w_pack, l), _Layer(hgain, l), mla_q_norm[l][None],
            mla_kv_norm[l][None])
        aq, ak = _fox_prep(misc, bf_rows[l][None], tri, batch, seq)
        qb, kb, vb = _mla_prep(qdn, kvdn, misc, cos_t, sin_t, _Layer(wq_slot, l), _Layer(wk_slot, l),
                               _Layer(wv, l), gslot, invcnt, gq_slot[l][None], gk_slot[l][None], gkr_slot[l][None], seq)
        ya = _prefix_attn(seq3(qa), seq3(ka), seq3(va), aq, ak, frame_causal=True, name="fox_attn")
        yb = _prefix_attn(seq3(qb), seq3(kb), seq3(vb), frame_causal=False, name="mla_attn")
        yc = _band_attn(seq3(qc), seq3(kc), seq3(vc), band_tables[l])
        flat = lambda a: a.reshape(batch * seq, BRANCH_WIDTH)
        xt = _merge(xt, flat(ya), flat(yb), flat(yc), norm_mix[l][None], _Layer(w_gate, l), b_gate[l][None],
                    _Layer(w_branch_b, l), _Layer(w_out_b, l))
        xt = _ffn(xt, norm_ffn[l][None], _Layer(w_ff1_b, l), _Layer(w_ff2_b, l))
    return xt.reshape(batch, seq, D_MODEL)
```

```python
import functools
from typing import NamedTuple

import numpy as np
import jax
import jax.numpy as jnp
from jax import lax
from jax.experimental import pallas as pl
from jax.experimental.pallas import tpu as pltpu

F32 = jnp.float32
BF16 = jnp.bfloat16

D_MODEL = 1024
DEPTH = 4
CHUNK = 64
EPS = 1e-6
NEG_INF = -1e30
HEADS = 8
HEAD_DIM = 64
B_NOPE = 64
B_ROPE = 32
B_Q_LORA = 384
B_KV_LORA = 256
ROPE_THETA = 10000.0
C_LEFT_CHUNKS = 8
REL_CLIP = 256
N_BRANCH = 3
BRANCH_WIDTH = 512
D_FF = 4 * D_MODEL
IN_SIZES = (512, 512, 512, HEADS, B_Q_LORA, B_KV_LORA, B_ROPE, 512, 512, 512, N_BRANCH * D_MODEL)
IN_OFFS = tuple(int(v) for v in np.cumsum((0,) + IN_SIZES))

LANES = 128
MXU_DIM = 256
SLOT = LANES
TOKEN_TILE = 1024
ATTN_TQ = 256
PAIRS_PER_STEP = 2
PIPE_DEPTH = 2
VMEM_LIMIT = 56 * 1024 * 1024

PACK_QD = 6 * 512
PACK_KVD = PACK_QD + B_Q_LORA
PACK_MISC = PACK_KVD + B_KV_LORA
PACK_WIDTH = PACK_MISC + LANES
MISC_KR = 64
LOG2E = float(np.log2(np.e))


def _dot(a, b):
    return jnp.dot(a, b, preferred_element_type=F32)


def _dot_nt(a, b):
    return lax.dot_general(a, b, (((1,), (1,)), ((), ())), preferred_element_type=F32)


def _split2(v):
    hi = v.astype(BF16)
    lo = (v - hi.astype(F32)).astype(BF16)
    return hi, lo


def _group_sumsq(u, g_ref):
    hi, lo = _split2(u * u)
    g = g_ref[...]
    return _dot(hi, g) + _dot(lo, g)


def _row_rms(x, gain):
    ms = jnp.mean(x * x, axis=-1, keepdims=True)
    return x * lax.rsqrt(ms + EPS) * gain


class _Layer(NamedTuple):
    stacked: jax.Array
    index: int


def _const_spec(op):
    if isinstance(op, _Layer):
        tail = op.stacked.shape[1:]
        return pl.BlockSpec((None,) + tail, lambda *_: (op.index,) + (0,) * len(tail),
                            pipeline_mode=pl.Buffered(1))
    return pl.BlockSpec(op.shape, lambda *_: (0,) * op.ndim, pipeline_mode=pl.Buffered(1))


def _arrays(ops):
    return [op.stacked if isinstance(op, _Layer) else op for op in ops]


def _params(n_axes):
    return pltpu.CompilerParams(dimension_semantics=("parallel",) * n_axes,
                                vmem_limit_bytes=VMEM_LIMIT)


def _in_proj_kernel(x_ref, gmix_ref, w_ref, hgain_ref, gq_ref, gkv_ref,
                    qa_ref, ka_ref, va_ref, qc_ref, kc_ref, vc_ref, qd_ref, kvd_ref, misc_ref):
    h = _row_rms(x_ref[...], gmix_ref[...]).astype(BF16)

    first_head = lax.broadcasted_iota(jnp.int32, (1, LANES), 1) < HEAD_DIM

    def head_normed(col, gain_row):
        u = _dot(h, w_ref[:, col:col + 512])
        parts = []
        for j in range(512 // LANES):
            ub = u[:, LANES * j:LANES * (j + 1)]
            sq = ub * ub
            s0 = jnp.sum(jnp.where(first_head, sq, 0.0), axis=-1, keepdims=True)
            s1 = jnp.sum(jnp.where(first_head, 0.0, sq), axis=-1, keepdims=True)
            parts.append(ub * lax.rsqrt(jnp.where(first_head, s0, s1) * (1.0 / HEAD_DIM) + EPS))
        return (jnp.concatenate(parts, axis=1) * hgain_ref[gain_row:gain_row + 1, :]).astype(BF16)

    qa_ref[...] = head_normed(0, 0)
    ka_ref[...] = head_normed(512, 1)
    va_ref[...] = _dot(h, w_ref[:, 1024:1536]).astype(BF16)
    qc_ref[...] = head_normed(1536, 2)
    kc_ref[...] = head_normed(2048, 3)
    vc_ref[...] = _dot(h, w_ref[:, 2560:3072]).astype(BF16)
    tail = _dot(h, w_ref[:, PACK_QD:PACK_WIDTH])
    qd_ref[...] = _row_rms(tail[:, :B_Q_LORA], gq_ref[...]).astype(BF16)
    kvd_ref[...] = _row_rms(tail[:, B_Q_LORA:B_Q_LORA + B_KV_LORA], gkv_ref[...]).astype(BF16)
    misc_ref[...] = tail[:, B_Q_LORA + B_KV_LORA:]


def _in_proj(x, gmix, w_pack, hgain, gq, gkv):
    t = x.shape[0]
    tm = TOKEN_TILE
    row = lambda width: pl.BlockSpec((tm, width), lambda i: (i, 0))
    out_shapes = [jax.ShapeDtypeStruct((t, 512), BF16)] * 6 + [
        jax.ShapeDtypeStruct((t, B_Q_LORA), BF16),
        jax.ShapeDtypeStruct((t, B_KV_LORA), BF16),
        jax.ShapeDtypeStruct((t, LANES), F32)]
    consts = [gmix, w_pack, hgain, gq, gkv]
    return pl.pallas_call(
        _in_proj_kernel,
        grid=(t // tm,),
        in_specs=[row(D_MODEL)] + [_const_spec(c) for c in consts],
        out_specs=[row(512)] * 6 + [row(B_Q_LORA), row(B_KV_LORA), row(LANES)],
        out_shape=out_shapes,
        compiler_params=_params(1),
        name="in_proj",
    )(x, *_arrays(consts))


def _split3(v):
    hi = v.astype(BF16)
    r1 = v - hi.astype(F32)
    mid = r1.astype(BF16)
    lo = (r1 - mid.astype(F32)).astype(BF16)
    return hi, mid, lo


AUG_LANES = LANES // HEADS


def _forget_gate_lanes(misc, gate_refs, aq_ref, ak_ref, carry_ref):
    bf_ref, tri_ref, pq_ref, pk_ref, oq_ref, ok_ref = gate_refs
    blk = MXU_DIM
    lane = lax.broadcasted_iota(jnp.int32, (blk, LANES), 1)
    tri = tri_ref[...]
    local = []
    for j in range(misc.shape[0] // blk):
        z = misc[j * blk:(j + 1) * blk, :] + bf_ref[...]
        lf = -(jnp.maximum(-z, 0.0) + jnp.log1p(jnp.exp(-jnp.abs(z))))
        hi, mid, lo = _split3(jnp.where(lane < HEADS, lf, 0.0))
        local.append(_dot(tri, hi) + _dot(tri, mid) + _dot(tri, lo))
    carry = carry_ref[...]
    for j, loc in enumerate(local):
        rows = slice(j * blk, (j + 1) * blk)
        c = loc + carry
        carry = c[blk - 1:blk, :]
        c3 = jnp.concatenate(_split3(c * LOG2E), axis=1)
        aq_ref[rows, :] = (_dot(c3, pq_ref[...]) + oq_ref[...]).astype(BF16)
        ak_ref[rows, :] = (_dot(c3, pk_ref[...]) + ok_ref[...]).astype(BF16)
    carry_ref[...] = carry


def _gate_constants(bf_row):
    pq = np.zeros((3 * LANES, LANES), np.float32)
    pk = np.zeros((3 * LANES, LANES), np.float32)
    oq = np.zeros((1, LANES), np.float32)
    ok = np.zeros((1, LANES), np.float32)
    for h in range(HEADS):
        for t in range(3):
            pq[t * LANES + h, h * AUG_LANES + t] = 1.0
            pk[t * LANES + h, h * AUG_LANES + 3 + t] = -1.0
            oq[0, h * AUG_LANES + 3 + t] = 1.0
            ok[0, h * AUG_LANES + t] = 1.0
    tri = jnp.asarray(np.tril(np.ones((MXU_DIM, MXU_DIM))), dtype=BF16)
    return [bf_row, tri, jnp.asarray(pq, BF16), jnp.asarray(pk, BF16), jnp.asarray(oq), jnp.asarray(ok)]


def _rope_slot(x, cos, sin, lo_half):
    partner = jnp.where(lo_half, pltpu.roll(x, LANES - B_ROPE // 2, 1), pltpu.roll(x, B_ROPE // 2, 1))
    return x * cos + partner * sin


def _mla_prep_kernel(qd_ref, kvd_ref, misc_ref, cos_ref, sin_ref, wq_ref, wk_ref, wv_ref,
                     gslot_ref, invcnt_ref, gq_ref, gk_ref, gkr_ref, *rest, tiles_per_seq):
    gate_refs, (q_ref, k_ref, v_ref, aq_ref, ak_ref, carry_ref) = rest[:6], rest[6:]

    @pl.when(pl.program_id(0) % tiles_per_seq == 0)
    def _():
        carry_ref[...] = jnp.zeros_like(carry_ref)

    _forget_gate_lanes(misc_ref[...], gate_refs, aq_ref, ak_ref, carry_ref)
    tm = qd_ref.shape[0]
    lane = lax.broadcasted_iota(jnp.int32, (tm, SLOT), 1)
    lo_half = lane < MISC_KR + B_ROPE // 2
    in_kr = (lane >= MISC_KR) & (lane < MISC_KR + B_ROPE)
    cos = cos_ref[...]
    sin = sin_ref[...]
    qd = qd_ref[...]
    kvd = kvd_ref[...]

    def slot_normed(u, gain_ref, col):
        ss = _group_sumsq(u, gslot_ref)
        return u * lax.rsqrt(ss * invcnt_ref[...] + EPS) * gain_ref[:, col:col + MXU_DIM]

    m = misc_ref[...]
    ss_kr = jnp.sum(jnp.where(in_kr, m * m, 0.0), axis=-1, keepdims=True)
    kr = _rope_slot(m * lax.rsqrt(ss_kr * (1.0 / B_ROPE) + EPS) * gkr_ref[...], cos, sin, lo_half)

    for j in range(HEADS * SLOT // MXU_DIM):
        col = j * MXU_DIM
        qn = slot_normed(_dot(qd, wq_ref[:, col:col + MXU_DIM]), gq_ref, col)
        kn = slot_normed(_dot(kvd, wk_ref[:, col:col + MXU_DIM]), gk_ref, col)
        for s in range(MXU_DIM // SLOT):
            sl = slice(s * SLOT, (s + 1) * SLOT)
            q_ref[:, col + s * SLOT:col + (s + 1) * SLOT] = _rope_slot(qn[:, sl], cos, sin, lo_half).astype(BF16)
            k_ref[:, col + s * SLOT:col + (s + 1) * SLOT] = (kn[:, sl] + kr).astype(BF16)
    v_ref[...] = _dot(kvd, wv_ref[...]).astype(BF16)


def _mla_prep(qdn, kvdn, misc, cos_t, sin_t, wq, wk, wv, gslot, invcnt, gq, gk, gkr, bf_row, seq):
    t = qdn.shape[0]
    tm = TOKEN_TILE
    per_seq = seq // tm
    row = lambda width: pl.BlockSpec((tm, width), lambda i: (i, 0))
    pos = pl.BlockSpec((tm, SLOT), lambda i: (i % per_seq, 0))
    consts = [wq, wk, wv, gslot, invcnt, gq, gk, gkr] + _gate_constants(bf_row)
    return pl.pallas_call(
        functools.partial(_mla_prep_kernel, tiles_per_seq=per_seq),
        grid=(t // tm,),
        in_specs=[row(B_Q_LORA), row(B_KV_LORA), row(LANES), pos, pos] + [_const_spec(c) for c in consts],
        out_specs=[row(HEADS * SLOT), row(HEADS * SLOT), row(BRANCH_WIDTH), row(LANES), row(LANES)],
        out_shape=[jax.ShapeDtypeStruct((t, HEADS * SLOT), BF16),
                   jax.ShapeDtypeStruct((t, HEADS * SLOT), BF16),
                   jax.ShapeDtypeStruct((t, BRANCH_WIDTH), BF16),
                   jax.ShapeDtypeStruct((t, LANES), BF16),
                   jax.ShapeDtypeStruct((t, LANES), BF16)],
        scratch_shapes=[pltpu.VMEM((1, LANES), F32)],
        compiler_params=pltpu.CompilerParams(dimension_semantics=("arbitrary",), vmem_limit_bytes=VMEM_LIMIT),
        name="mla_prep",
    )(qdn, kvdn, misc, cos_t, sin_t, *_arrays(consts))


def _head_lanes(lane, hh):
    return (lane >= hh * HEAD_DIM) & (lane < (hh + 1) * HEAD_DIM)


def _longest_first(n):
    return list(range(n - 1, -1, -1))


def _run_units(units, s_ref, o_ref):
    depth = s_ref.shape[0]
    lane = lax.broadcasted_iota(jnp.int32, (1, LANES), 1)
    stats = {}

    def scores(n):
        pieces = units[n][0]()
        for c0, blk in pieces:
            s_ref[n % depth, :, c0:c0 + blk.shape[1]] = blk
        stats[n] = (functools.reduce(jnp.maximum, [jnp.max(blk, axis=-1, keepdims=True) for _, blk in pieces]),
                    sum(blk.shape[1] for _, blk in pieces))

    def finish(n):
        _, values, rows, head = units[n]
        m, klen = stats.pop(n)
        p = jnp.exp2(s_ref[n % depth, :, 0:klen] - m)
        l = jnp.sum(p, axis=-1, keepdims=True)
        o = (_dot(p.astype(BF16), values(0, klen)) / l).astype(BF16)
        cols = slice(head // 2 * LANES, (head // 2 + 1) * LANES)
        if head % 2 == 0:
            o_ref[0, rows, cols] = o
        else:
            o_ref[0, rows, cols] = jnp.where(lane < HEAD_DIM, o_ref[0, rows, cols], o)

    for n in range(min(depth, len(units))):
        scores(n)
    for n in range(len(units)):
        finish(n)
        if n + depth < len(units):
            scores(n + depth)


def _prefix_attn_kernel(*refs, frame_causal, decay):
    heads = 2 * PAIRS_PER_STEP
    if decay:
        q_ref, k_ref, v_ref, aq_ref, ak_ref, o_ref, s_ref, kx_ref = refs
        for pp in range(PAIRS_PER_STEP):
            kx_ref[pp, :, 0:LANES] = k_ref[0, :, pp * LANES:(pp + 1) * LANES]
            kx_ref[pp, :, LANES:2 * LANES] = ak_ref[0]
    else:
        q_ref, k_ref, v_ref, o_ref, s_ref = refs
    seq = q_ref.shape[1]
    tq = ATTN_TQ
    lane = lax.broadcasted_iota(jnp.int32, (1, LANES), 1)
    row = lax.broadcasted_iota(jnp.int32, (tq, tq), 0)
    col = lax.broadcasted_iota(jnp.int32, (tq, tq), 1)
    if frame_causal:
        diag_ok = col <= row
    else:
        diag_ok = (col // CHUNK) <= (row // CHUNK)

    def unit(i, head):
        q0, q1 = i * tq, (i + 1) * tq
        pp, hh = divmod(head, 2)

        def score_blocks():
            if decay:
                q2 = q_ref[0, q0:q1, pp * LANES:(pp + 1) * LANES]
                a2 = aq_ref[0, q0:q1, :]
                aug0 = (heads * pl.program_id(1) + head) * AUG_LANES
                qh = jnp.concatenate(
                    [jnp.where(_head_lanes(lane, hh), q2, jnp.zeros_like(q2)),
                     jnp.where((lane >= aug0) & (lane < aug0 + AUG_LANES), a2, jnp.zeros_like(a2))], axis=1)
                k_of = lambda a, b: kx_ref[pp, a:b, :]
            else:
                qh = q_ref[0, q0:q1, head * SLOT:(head + 1) * SLOT]
                k_of = lambda a, b: k_ref[0, a:b, head * SLOT:(head + 1) * SLOT]
            pieces = [(0, _dot_nt(qh, k_of(0, q0)))] if i else []
            return pieces + [(q0, jnp.where(diag_ok, _dot_nt(qh, k_of(q0, q1)), NEG_INF))]

        return (score_blocks, lambda a, b: v_ref[0, a:b, pp * LANES:(pp + 1) * LANES], slice(q0, q1), head)

    _run_units([unit(i, head) for i in _longest_first(seq // tq) for head in range(heads)], s_ref, o_ref)


def _prefix_attn(q, k, v, aq=None, ak=None, *, frame_causal, name):
    batch, seq, _ = v.shape
    decay = aq is not None
    vw = PAIRS_PER_STEP * LANES
    qk_w = vw if decay else 2 * PAIRS_PER_STEP * SLOT
    group_spec = lambda w: pl.BlockSpec((1, seq, w), lambda b, g: (b, 0, g))
    in_specs = [group_spec(qk_w), group_spec(qk_w), group_spec(vw)]
    args = [q, k, v]
    scratch = [pltpu.VMEM((PIPE_DEPTH, ATTN_TQ, seq), F32)]
    if decay:
        aug_spec = pl.BlockSpec((1, seq, LANES), lambda b, g: (b, 0, 0))
        in_specs += [aug_spec, aug_spec]
        args += [aq, ak]
        scratch += [pltpu.VMEM((PAIRS_PER_STEP, seq, 2 * LANES), BF16)]
    return pl.pallas_call(
        functools.partial(_prefix_attn_kernel, frame_causal=frame_causal, decay=decay),
        grid=(batch, HEADS // (2 * PAIRS_PER_STEP)),
        in_specs=in_specs,
        out_specs=group_spec(vw),
        out_shape=jax.ShapeDtypeStruct((batch, seq, BRANCH_WIDTH), BF16),
        scratch_shapes=scratch,
        compiler_params=_params(2),
        name=name,
    )(*args)


BAND_BLOCKS = C_LEFT_CHUNKS * CHUNK // ATTN_TQ + 1
BAND_WIDTH = BAND_BLOCKS * ATTN_TQ
BAND_PERIOD = BAND_WIDTH + ATTN_TQ


def _band_attn_kernel(q_ref, k_ref, v_ref, w_ref, o_ref, bias_ref, s_ref):
    heads = 2 * PAIRS_PER_STEP
    seq = q_ref.shape[1]
    tq = ATTN_TQ
    lane = lax.broadcasted_iota(jnp.int32, (1, LANES), 1)

    @pl.when(pl.program_id(1) == 0)
    def _():
        r = lax.broadcasted_iota(jnp.int32, (tq, BAND_WIDTH), 0)
        c = lax.broadcasted_iota(jnp.int32, (tq, BAND_WIDTH), 1)
        dchunk = r // CHUNK - (c // CHUNK - C_LEFT_CHUNKS)
        valid = (dchunk >= 0) & (dchunk <= C_LEFT_CHUNKS)
        for head in range(heads):
            table = jnp.broadcast_to(w_ref[head], (tq, BAND_PERIOD))
            toeplitz = pltpu.roll(table, 0, 1, stride=1, stride_axis=0)
            bias_ref[head] = jnp.where(valid, toeplitz[:, :BAND_WIDTH], NEG_INF)

    def unit(t, head):
        k0, k1 = max(0, t - (BAND_BLOCKS - 1)) * tq, (t + 1) * tq
        boff = BAND_WIDTH - (k1 - k0)
        pp, hh = divmod(head, 2)
        cols = slice(pp * LANES, (pp + 1) * LANES)

        def score_blocks():
            q2 = q_ref[0, t * tq:(t + 1) * tq, cols]
            qh = jnp.where(_head_lanes(lane, hh), q2, jnp.zeros_like(q2))
            return [(0, _dot_nt(qh, k_ref[0, k0:k1, cols]) + bias_ref[head, :, boff:])]

        return (score_blocks, lambda a, b: v_ref[0, k0 + a:k0 + b, cols], slice(t * tq, (t + 1) * tq), head)

    _run_units([unit(t, head) for t in _longest_first(seq // tq) for head in range(heads)], s_ref, o_ref)


def _band_attn(q, k, v, w):
    batch, seq, _ = v.shape
    heads = 2 * PAIRS_PER_STEP
    group_spec = pl.BlockSpec((1, seq, PAIRS_PER_STEP * LANES), lambda g, b: (b, 0, g))
    return pl.pallas_call(
        _band_attn_kernel,
        grid=(HEADS // heads, batch),
        in_specs=[group_spec, group_spec, group_spec,
                  pl.BlockSpec((heads, 1, BAND_PERIOD), lambda g, b: (g, 0, 0))],
        out_specs=group_spec,
        out_shape=jax.ShapeDtypeStruct((batch, seq, BRANCH_WIDTH), BF16),
        scratch_shapes=[pltpu.VMEM((heads, ATTN_TQ, BAND_WIDTH), F32),
                        pltpu.VMEM((PIPE_DEPTH, ATTN_TQ, BAND_WIDTH), F32)],
        compiler_params=pltpu.CompilerParams(dimension_semantics=("parallel", "arbitrary"),
                                             vmem_limit_bytes=VMEM_LIMIT),
        name="band_attn",
    )(q, k, v, w)


def _merge_kernel(x_ref, ya_ref, yb_ref, yc_ref, gmix_ref, wg_ref, bg_ref, wb_ref, wo_ref, o_ref):
    x = x_ref[...]
    h = _row_rms(x, gmix_ref[...]).astype(BF16)
    merged = None
    for n, y_ref in enumerate((ya_ref, yb_ref, yc_ref)):
        cols = slice(n * D_MODEL, (n + 1) * D_MODEL)
        gate = jax.nn.sigmoid(_dot(h, wg_ref[:, cols]) + bg_ref[:, cols])
        term = gate * _dot(y_ref[...], wb_ref[n])
        merged = term if merged is None else merged + term
    o_ref[...] = x + _dot(merged.astype(BF16), wo_ref[...])


def _merge(x, ya, yb, yc, gmix, wg, bg, wb, wo):
    t = x.shape[0]
    tm = TOKEN_TILE
    row = lambda width: pl.BlockSpec((tm, width), lambda i: (i, 0))
    consts = [gmix, wg, bg, wb, wo]
    return pl.pallas_call(
        _merge_kernel,
        grid=(t // tm,),
        in_specs=[row(D_MODEL)] + [row(BRANCH_WIDTH)] * 3 + [_const_spec(c) for c in consts],
        out_specs=row(D_MODEL),
        out_shape=jax.ShapeDtypeStruct((t, D_MODEL), F32),
        compiler_params=_params(1),
        name="merge",
    )(x, ya, yb, yc, *_arrays(consts))


FF_CHUNK = 1024


def _ffn_kernel(x_ref, g_ref, w1_ref, w2_ref, o_ref):
    x = x_ref[...]
    h = _row_rms(x, g_ref[...]).astype(BF16)
    acc = x
    for c in range(D_FF // FF_CHUNK):
        a = jnp.maximum(_dot(h, w1_ref[:, c * FF_CHUNK:(c + 1) * FF_CHUNK]), 0.0)
        acc = acc + _dot((a * a).astype(BF16), w2_ref[c * FF_CHUNK:(c + 1) * FF_CHUNK, :])
    o_ref[...] = acc


def _ffn(x, g, w1, w2):
    t = x.shape[0]
    tm = TOKEN_TILE
    row = pl.BlockSpec((tm, D_MODEL), lambda i: (i, 0))
    return pl.pallas_call(
        _ffn_kernel,
        grid=(t // tm,),
        in_specs=[row] + [_const_spec(c) for c in (g, w1, w2)],
        out_specs=row,
        out_shape=jax.ShapeDtypeStruct((t, D_MODEL), F32),
        compiler_params=_params(1),
        name="ffn",
    )(x, *_arrays((g, w1, w2)))


def _group_matrix(groups):
    ids = np.asarray(groups)
    return jnp.asarray((ids[:, None] == ids[None, :]) & (ids[:, None] >= 0), dtype=BF16)


def _band_table(rel_bias):
    rb = rel_bias.astype(F32) * LOG2E
    left = BAND_WIDTH - ATTN_TQ - REL_CLIP
    far = jnp.broadcast_to(rb[..., 2 * REL_CLIP:], rb.shape[:-1] + (BAND_PERIOD,))
    w = jnp.concatenate([far[..., :left], rb[..., ::-1], far[..., left + 2 * REL_CLIP + 1:]], axis=-1)
    return w[..., None, :]


def _rope_tables(seq):
    half = B_ROPE // 2
    inv = ROPE_THETA ** (-jnp.arange(half, dtype=F32) / half)
    ang = jnp.arange(seq).astype(F32)[:, None] * inv[None, :]
    cos, sin = jnp.cos(ang), jnp.sin(ang)
    pad_l = jnp.ones((seq, MISC_KR), F32)
    pad_r = jnp.ones((seq, SLOT - MISC_KR - B_ROPE), F32)
    cos_t = jnp.concatenate([pad_l, cos, cos, pad_r], axis=1)
    sin_t = jnp.concatenate([0 * pad_l, -sin, sin, 0 * pad_r], axis=1)
    return cos_t, sin_t


def kernel(x, norm_mix, w_in, b_forget, b_gate, qk_norm_a, mla_q_norm, mla_kv_norm, w_q_up, w_kv_up,
           qk_norm_b_nope, qk_norm_b_rope, qk_norm_c, rel_bias, w_branch, w_out, norm_ffn, w_ff1, w_ff2):
    batch, seq, _ = x.shape
    depth = w_in.shape[0]
    o = IN_OFFS
    zeros = lambda *s: jnp.zeros(s, F32)

    misc_w = jnp.concatenate([w_in[:, :, o[3]:o[4]], zeros(depth, D_MODEL, MISC_KR - HEADS),
                              w_in[:, :, o[6]:o[7]], zeros(depth, D_MODEL, SLOT - MISC_KR - B_ROPE)], axis=2)
    w_pack = jnp.concatenate([w_in[:, :, o[0]:o[3]], w_in[:, :, o[7]:o[10]], w_in[:, :, o[4]:o[6]], misc_w],
                             axis=2).astype(BF16)
    w_gate = w_in[:, :, o[10]:o[11]].astype(BF16)

    attn_scale = HEAD_DIM ** -0.5 * LOG2E
    tile8 = lambda g: jnp.tile(g, (1, HEADS))
    hgain = jnp.stack([tile8(qk_norm_a[:, 0]) * attn_scale, tile8(qk_norm_a[:, 1]),
                       tile8(qk_norm_c[:, 0]) * attn_scale, tile8(qk_norm_c[:, 1])], axis=1)

    mla_scale = (B_NOPE + B_ROPE) ** -0.5 * LOG2E
    pad = SLOT - B_NOPE - B_ROPE
    wq_slot = jnp.pad(w_q_up.reshape(depth, B_Q_LORA, HEADS, B_NOPE + B_ROPE),
                      ((0, 0), (0, 0), (0, 0), (0, pad))).reshape(depth, B_Q_LORA, HEADS * SLOT).astype(BF16)
    wkv = w_kv_up.reshape(depth, B_KV_LORA, HEADS, 2 * B_NOPE)
    wk_slot = jnp.pad(wkv[..., :B_NOPE], ((0, 0), (0, 0), (0, 0), (0, SLOT - B_NOPE))
                      ).reshape(depth, B_KV_LORA, HEADS * SLOT).astype(BF16)
    wv = wkv[..., B_NOPE:].reshape(depth, B_KV_LORA, BRANCH_WIDTH).astype(BF16)
    gq_slot = tile8(jnp.concatenate([qk_norm_b_nope[:, 0], qk_norm_b_rope[:, 0], zeros(depth, pad)], axis=1)
                    ) * mla_scale
    gk_slot = tile8(jnp.concatenate([qk_norm_b_nope[:, 1], zeros(depth, SLOT - B_NOPE)], axis=1))
    gkr_slot = jnp.concatenate([zeros(depth, MISC_KR), qk_norm_b_rope[:, 1], zeros(depth, pad)], axis=1)
    slot_groups = np.concatenate([np.zeros(B_NOPE), np.ones(B_ROPE), -np.ones(pad)]).astype(np.int64)
    slot_groups2 = np.concatenate([slot_groups, np.where(slot_groups >= 0, slot_groups + 2, -1)])
    gslot = _group_matrix(slot_groups2)
    invcnt = jnp.asarray(np.tile(np.concatenate([np.full(B_NOPE, 1.0 / B_NOPE), np.full(B_ROPE, 1.0 / B_ROPE),
                                                  np.ones(pad)]), 2)[None, :], dtype=F32)
    cos_t, sin_t = _rope_tables(seq)
    bf_rows = jnp.pad(b_forget.astype(F32), ((0, 0), (0, LANES - HEADS)))
    band_tables = _band_table(rel_bias)

    w_branch_b = w_branch.astype(BF16)
    w_out_b = w_out.astype(BF16)
    w_ff1_b = w_ff1.astype(BF16)
    w_ff2_b = w_ff2.astype(BF16)

    xt = x.reshape(batch * seq, D_MODEL)
    seq3 = lambda a: a.reshape(batch, seq, a.shape[-1])
    for l in range(depth):
        qa, ka, va, qc, kc, vc, qdn, kvdn, misc = _in_proj(
            xt, norm_mix[l][None], _Layer(w_pack, l), _Layer(hgain, l), mla_q_norm[l][None],
            mla_kv_norm[l][None])
        qb, kb, vb, aq, ak = _mla_prep(
            qdn, kvdn, misc, cos_t, sin_t, _Layer(wq_slot, l), _Layer(wk_slot, l), _Layer(wv, l), gslot, invcnt,
            gq_slot[l][None], gk_slot[l][None], gkr_slot[l][None], bf_rows[l][None], seq)
        ya = _prefix_attn(seq3(qa), seq3(ka), seq3(va), seq3(aq), seq3(ak), frame_causal=True, name="fox_attn")
        yb = _prefix_attn(seq3(qb), seq3(kb), seq3(vb), frame_causal=False, name="mla_attn")
        yc = _band_attn(seq3(qc), seq3(kc), seq3(vc), band_tables[l])
        flat = lambda a: a.reshape(batch * seq, BRANCH_WIDTH)
        xt = _merge(xt, flat(ya), flat(yb), flat(yc), norm_mix[l][None], _Layer(w_gate, l), b_gate[l][None],
                    _Layer(w_branch_b, l), _Layer(w_out_b, l))
        xt = _ffn(xt, norm_ffn[l][None], _Layer(w_ff1_b, l), _Layer(w_ff2_b, l))
    return xt.reshape(batch, seq, D_MODEL)
```

```python
import functools
from typing import NamedTuple

import numpy as np
import jax
import jax.numpy as jnp
from jax import lax
from jax.experimental import pallas as pl
from jax.experimental.pallas import tpu as pltpu

F32 = jnp.float32
BF16 = jnp.bfloat16

D_MODEL = 1024
DEPTH = 4
CHUNK = 64
EPS = 1e-6
NEG_INF = -1e30
HEADS = 8
HEAD_DIM = 64
B_NOPE = 64
B_ROPE = 32
B_Q_LORA = 384
B_KV_LORA = 256
ROPE_THETA = 10000.0
C_LEFT_CHUNKS = 8
REL_CLIP = 256
N_BRANCH = 3
BRANCH_WIDTH = 512
D_FF = 4 * D_MODEL
IN_SIZES = (512, 512, 512, HEADS, B_Q_LORA, B_KV_LORA, B_ROPE, 512, 512, 512, N_BRANCH * D_MODEL)
IN_OFFS = tuple(int(v) for v in np.cumsum((0,) + IN_SIZES))

LANES = 128
MXU_DIM = 256
SLOT = LANES
TOKEN_TILE = 1024
ATTN_TQ = 256
PAIRS_PER_STEP = 2
PIPE_DEPTH = 2
VMEM_LIMIT = 56 * 1024 * 1024

MISC_KR = 64
LOG2E = float(np.log2(np.e))


def _dot(a, b):
    return jnp.dot(a, b, preferred_element_type=F32)


def _dot_nt(a, b):
    return lax.dot_general(a, b, (((1,), (1,)), ((), ())), preferred_element_type=F32)


def _split2(v):
    hi = v.astype(BF16)
    lo = (v - hi.astype(F32)).astype(BF16)
    return hi, lo


def _group_sumsq(u, g_ref):
    hi, lo = _split2(u * u)
    g = g_ref[...]
    return _dot(hi, g) + _dot(lo, g)


def _row_rms(x, gain):
    ms = jnp.mean(x * x, axis=-1, keepdims=True)
    return x * lax.rsqrt(ms + EPS) * gain


class _Layer(NamedTuple):
    stacked: jax.Array
    index: int
    rows: int = 0


def _const_spec(op):
    if isinstance(op, _Layer):
        tail = op.stacked.shape[1:]
        if op.rows:
            tail = (op.rows,) + tail[1:]
        return pl.BlockSpec((None,) + tail, lambda *_: (op.index,) + (0,) * len(tail),
                            pipeline_mode=pl.Buffered(1))
    return pl.BlockSpec(op.shape, lambda *_: (0,) * op.ndim, pipeline_mode=pl.Buffered(1))


def _arrays(ops):
    return [op.stacked if isinstance(op, _Layer) else op for op in ops]


def _params(n_axes):
    return pltpu.CompilerParams(dimension_semantics=("parallel",) * n_axes,
                                vmem_limit_bytes=VMEM_LIMIT)


def _in_proj_kernel(x_ref, gmix_ref, wt_ref, hgain_ref, gq_ref, gkv_ref,
                    qa_ref, ka_ref, va_ref, qc_ref, kc_ref, vc_ref, qd_ref, kvd_ref, misc_ref):
    h = _row_rms(x_ref[...], gmix_ref[...]).astype(BF16)
    o = IN_OFFS

    first_head = lax.broadcasted_iota(jnp.int32, (1, LANES), 1) < HEAD_DIM

    def project(a, b):
        return _dot_nt(h, wt_ref[a:b, :].astype(BF16))

    def head_normed(a, gain_row):
        u = project(a, a + 512)
        parts = []
        for j in range(512 // LANES):
            ub = u[:, LANES * j:LANES * (j + 1)]
            sq = ub * ub
            s0 = jnp.sum(jnp.where(first_head, sq, 0.0), axis=-1, keepdims=True)
            s1 = jnp.sum(jnp.where(first_head, 0.0, sq), axis=-1, keepdims=True)
            parts.append(ub * lax.rsqrt(jnp.where(first_head, s0, s1) * (1.0 / HEAD_DIM) + EPS))
        return (jnp.concatenate(parts, axis=1) * hgain_ref[gain_row:gain_row + 1, :]).astype(BF16)

    qa_ref[...] = head_normed(o[0], 0)
    ka_ref[...] = head_normed(o[1], 1)
    va_ref[...] = project(o[2], o[3]).astype(BF16)
    qc_ref[...] = head_normed(o[7], 2)
    kc_ref[...] = head_normed(o[8], 3)
    vc_ref[...] = project(o[9], o[10]).astype(BF16)
    zrows = lambda n: jnp.zeros((n, D_MODEL), F32)
    w_tail = jnp.concatenate([wt_ref[o[4]:o[6], :], wt_ref[o[3]:o[4], :], zrows(MISC_KR - HEADS),
                              wt_ref[o[6]:o[7], :], zrows(LANES - MISC_KR - B_ROPE)], axis=0)
    tail = _dot_nt(h, w_tail.astype(BF16))
    qd_ref[...] = _row_rms(tail[:, :B_Q_LORA], gq_ref[...]).astype(BF16)
    kvd_ref[...] = _row_rms(tail[:, B_Q_LORA:B_Q_LORA + B_KV_LORA], gkv_ref[...]).astype(BF16)
    misc_ref[...] = tail[:, B_Q_LORA + B_KV_LORA:]


def _in_proj(x, gmix, w_t, hgain, gq, gkv):
    t = x.shape[0]
    tm = TOKEN_TILE
    row = lambda width: pl.BlockSpec((tm, width), lambda i: (i, 0))
    out_shapes = [jax.ShapeDtypeStruct((t, 512), BF16)] * 6 + [
        jax.ShapeDtypeStruct((t, B_Q_LORA), BF16),
        jax.ShapeDtypeStruct((t, B_KV_LORA), BF16),
        jax.ShapeDtypeStruct((t, LANES), F32)]
    consts = [gmix, w_t, hgain, gq, gkv]
    return pl.pallas_call(
        _in_proj_kernel,
        grid=(t // tm,),
        in_specs=[row(D_MODEL)] + [_const_spec(c) for c in consts],
        out_specs=[row(512)] * 6 + [row(B_Q_LORA), row(B_KV_LORA), row(LANES)],
        out_shape=out_shapes,
        compiler_params=_params(1),
        name="in_proj",
    )(x, *_arrays(consts))


def _split3(v):
    hi = v.astype(BF16)
    r1 = v - hi.astype(F32)
    mid = r1.astype(BF16)
    lo = (r1 - mid.astype(F32)).astype(BF16)
    return hi, mid, lo


AUG_LANES = LANES // HEADS


def _forget_gate_lanes(misc, gate_refs, aq_ref, ak_ref, carry_ref):
    bf_ref, tri_ref, pq_ref, pk_ref, oq_ref, ok_ref = gate_refs
    blk = MXU_DIM
    lane = lax.broadcasted_iota(jnp.int32, (blk, LANES), 1)
    tri = tri_ref[...]
    local = []
    for j in range(misc.shape[0] // blk):
        z = misc[j * blk:(j + 1) * blk, :] + bf_ref[...]
        lf = -(jnp.maximum(-z, 0.0) + jnp.log1p(jnp.exp(-jnp.abs(z))))
        hi, mid, lo = _split3(jnp.where(lane < HEADS, lf, 0.0))
        local.append(_dot(tri, hi) + _dot(tri, mid) + _dot(tri, lo))
    carry = carry_ref[...]
    for j, loc in enumerate(local):
        rows = slice(j * blk, (j + 1) * blk)
        c = loc + carry
        carry = c[blk - 1:blk, :]
        c3 = jnp.concatenate(_split3(c * LOG2E), axis=1)
        aq_ref[rows, :] = (_dot(c3, pq_ref[...]) + oq_ref[...]).astype(BF16)
        ak_ref[rows, :] = (_dot(c3, pk_ref[...]) + ok_ref[...]).astype(BF16)
    carry_ref[...] = carry


def _gate_constants(bf_row):
    pq = np.zeros((3 * LANES, LANES), np.float32)
    pk = np.zeros((3 * LANES, LANES), np.float32)
    oq = np.zeros((1, LANES), np.float32)
    ok = np.zeros((1, LANES), np.float32)
    for h in range(HEADS):
        for t in range(3):
            pq[t * LANES + h, h * AUG_LANES + t] = 1.0
            pk[t * LANES + h, h * AUG_LANES + 3 + t] = -1.0
            oq[0, h * AUG_LANES + 3 + t] = 1.0
            ok[0, h * AUG_LANES + t] = 1.0
    tri = jnp.asarray(np.tril(np.ones((MXU_DIM, MXU_DIM))), dtype=BF16)
    return [bf_row, tri, jnp.asarray(pq, BF16), jnp.asarray(pk, BF16), jnp.asarray(oq), jnp.asarray(ok)]


def _rope_slot(x, cos, sin, lo_half):
    partner = jnp.where(lo_half, pltpu.roll(x, LANES - B_ROPE // 2, 1), pltpu.roll(x, B_ROPE // 2, 1))
    return x * cos + partner * sin


def _mla_prep_kernel(qd_ref, kvd_ref, misc_ref, cos_ref, sin_ref, wq_ref, wk_ref, wv_ref,
                     gslot_ref, invcnt_ref, gq_ref, gk_ref, gkr_ref, *rest, tiles_per_seq):
    gate_refs, (q_ref, k_ref, v_ref, aq_ref, ak_ref, carry_ref) = rest[:6], rest[6:]

    @pl.when(pl.program_id(0) % tiles_per_seq == 0)
    def _():
        carry_ref[...] = jnp.zeros_like(carry_ref)

    _forget_gate_lanes(misc_ref[...], gate_refs, aq_ref, ak_ref, carry_ref)
    tm = qd_ref.shape[0]
    lane = lax.broadcasted_iota(jnp.int32, (tm, SLOT), 1)
    lo_half = lane < MISC_KR + B_ROPE // 2
    in_kr = (lane >= MISC_KR) & (lane < MISC_KR + B_ROPE)
    cos = cos_ref[...]
    sin = sin_ref[...]
    qd = qd_ref[...]
    kvd = kvd_ref[...]

    def slot_normed(u, gain_ref, col):
        ss = _group_sumsq(u, gslot_ref)
        return u * lax.rsqrt(ss * invcnt_ref[...] + EPS) * gain_ref[:, col:col + MXU_DIM]

    m = misc_ref[...]
    ss_kr = jnp.sum(jnp.where(in_kr, m * m, 0.0), axis=-1, keepdims=True)
    kr = _rope_slot(m * lax.rsqrt(ss_kr * (1.0 / B_ROPE) + EPS) * gkr_ref[...], cos, sin, lo_half)

    for j in range(HEADS * SLOT // MXU_DIM):
        col = j * MXU_DIM
        qn = slot_normed(_dot(qd, wq_ref[:, col:col + MXU_DIM]), gq_ref, col)
        kn = slot_normed(_dot(kvd, wk_ref[:, col:col + MXU_DIM]), gk_ref, col)
        for s in range(MXU_DIM // SLOT):
            sl = slice(s * SLOT, (s + 1) * SLOT)
            q_ref[:, col + s * SLOT:col + (s + 1) * SLOT] = _rope_slot(qn[:, sl], cos, sin, lo_half).astype(BF16)
            k_ref[:, col + s * SLOT:col + (s + 1) * SLOT] = (kn[:, sl] + kr).astype(BF16)
    v_ref[...] = _dot(kvd, wv_ref[...]).astype(BF16)


def _mla_prep(qdn, kvdn, misc, cos_t, sin_t, wq, wk, wv, gslot, invcnt, gq, gk, gkr, bf_row, seq):
    t = qdn.shape[0]
    tm = TOKEN_TILE
    per_seq = seq // tm
    row = lambda width: pl.BlockSpec((tm, width), lambda i: (i, 0))
    pos = pl.BlockSpec((tm, SLOT), lambda i: (i % per_seq, 0))
    consts = [wq, wk, wv, gslot, invcnt, gq, gk, gkr] + _gate_constants(bf_row)
    return pl.pallas_call(
        functools.partial(_mla_prep_kernel, tiles_per_seq=per_seq),
        grid=(t // tm,),
        in_specs=[row(B_Q_LORA), row(B_KV_LORA), row(LANES), pos, pos] + [_const_spec(c) for c in consts],
        out_specs=[row(HEADS * SLOT), row(HEADS * SLOT), row(BRANCH_WIDTH), row(LANES), row(LANES)],
        out_shape=[jax.ShapeDtypeStruct((t, HEADS * SLOT), BF16),
                   jax.ShapeDtypeStruct((t, HEADS * SLOT), BF16),
                   jax.ShapeDtypeStruct((t, BRANCH_WIDTH), BF16),
                   jax.ShapeDtypeStruct((t, LANES), BF16),
                   jax.ShapeDtypeStruct((t, LANES), BF16)],
        scratch_shapes=[pltpu.VMEM((1, LANES), F32)],
        compiler_params=pltpu.CompilerParams(dimension_semantics=("arbitrary",), vmem_limit_bytes=VMEM_LIMIT),
        name="mla_prep",
    )(qdn, kvdn, misc, cos_t, sin_t, *_arrays(consts))


def _head_lanes(lane, hh):
    return (lane >= hh * HEAD_DIM) & (lane < (hh + 1) * HEAD_DIM)


def _longest_first(n):
    return list(range(n - 1, -1, -1))


def _run_units(units, s_ref, o_ref):
    depth = s_ref.shape[0]
    lane = lax.broadcasted_iota(jnp.int32, (1, LANES), 1)
    stats = {}

    def scores(n):
        pieces = units[n][0]()
        for c0, blk in pieces:
            s_ref[n % depth, :, c0:c0 + blk.shape[1]] = blk
        stats[n] = (functools.reduce(jnp.maximum, [jnp.max(blk, axis=-1, keepdims=True) for _, blk in pieces]),
                    sum(blk.shape[1] for _, blk in pieces))

    def finish(n):
        _, values, rows, head = units[n]
        m, klen = stats.pop(n)
        p = jnp.exp2(s_ref[n % depth, :, 0:klen] - m)
        l = jnp.sum(p, axis=-1, keepdims=True)
        o = (_dot(p.astype(BF16), values(0, klen)) / l).astype(BF16)
        cols = slice(head // 2 * LANES, (head // 2 + 1) * LANES)
        if head % 2 == 0:
            o_ref[0, rows, cols] = o
        else:
            o_ref[0, rows, cols] = jnp.where(lane < HEAD_DIM, o_ref[0, rows, cols], o)

    for n in range(min(depth, len(units))):
        scores(n)
    for n in range(len(units)):
        finish(n)
        if n + depth < len(units):
            scores(n + depth)


def _prefix_attn_kernel(*refs, frame_causal, decay):
    heads = 2 * PAIRS_PER_STEP
    if decay:
        q_ref, k_ref, v_ref, aq_ref, ak_ref, o_ref, s_ref, kx_ref = refs
        for pp in range(PAIRS_PER_STEP):
            kx_ref[pp, :, 0:LANES] = k_ref[0, :, pp * LANES:(pp + 1) * LANES]
            kx_ref[pp, :, LANES:2 * LANES] = ak_ref[0]
    else:
        q_ref, k_ref, v_ref, o_ref, s_ref = refs
    seq = q_ref.shape[1]
    tq = ATTN_TQ
    lane = lax.broadcasted_iota(jnp.int32, (1, LANES), 1)
    row = lax.broadcasted_iota(jnp.int32, (tq, tq), 0)
    col = lax.broadcasted_iota(jnp.int32, (tq, tq), 1)
    if frame_causal:
        diag_ok = col <= row
    else:
        diag_ok = (col // CHUNK) <= (row // CHUNK)

    def unit(i, head):
        q0, q1 = i * tq, (i + 1) * tq
        pp, hh = divmod(head, 2)

        def score_blocks():
            if decay:
                q2 = q_ref[0, q0:q1, pp * LANES:(pp + 1) * LANES]
                a2 = aq_ref[0, q0:q1, :]
                aug0 = (heads * pl.program_id(1) + head) * AUG_LANES
                qh = jnp.concatenate(
                    [jnp.where(_head_lanes(lane, hh), q2, jnp.zeros_like(q2)),
                     jnp.where((lane >= aug0) & (lane < aug0 + AUG_LANES), a2, jnp.zeros_like(a2))], axis=1)
                k_of = lambda a, b: kx_ref[pp, a:b, :]
            else:
                qh = q_ref[0, q0:q1, head * SLOT:(head + 1) * SLOT]
                k_of = lambda a, b: k_ref[0, a:b, head * SLOT:(head + 1) * SLOT]
            pieces = [(0, _dot_nt(qh, k_of(0, q0)))] if i else []
            return pieces + [(q0, jnp.where(diag_ok, _dot_nt(qh, k_of(q0, q1)), NEG_INF))]

        return (score_blocks, lambda a, b: v_ref[0, a:b, pp * LANES:(pp + 1) * LANES], slice(q0, q1), head)

    _run_units([unit(i, head) for i in _longest_first(seq // tq) for head in range(heads)], s_ref, o_ref)


def _prefix_attn(q, k, v, aq=None, ak=None, *, frame_causal, name):
    batch, seq, _ = v.shape
    decay = aq is not None
    vw = PAIRS_PER_STEP * LANES
    qk_w = vw if decay else 2 * PAIRS_PER_STEP * SLOT
    group_spec = lambda w: pl.BlockSpec((1, seq, w), lambda b, g: (b, 0, g))
    in_specs = [group_spec(qk_w), group_spec(qk_w), group_spec(vw)]
    args = [q, k, v]
    scratch = [pltpu.VMEM((PIPE_DEPTH, ATTN_TQ, seq), F32)]
    if decay:
        aug_spec = pl.BlockSpec((1, seq, LANES), lambda b, g: (b, 0, 0))
        in_specs += [aug_spec, aug_spec]
        args += [aq, ak]
        scratch += [pltpu.VMEM((PAIRS_PER_STEP, seq, 2 * LANES), BF16)]
    return pl.pallas_call(
        functools.partial(_prefix_attn_kernel, frame_causal=frame_causal, decay=decay),
        grid=(batch, HEADS // (2 * PAIRS_PER_STEP)),
        in_specs=in_specs,
        out_specs=group_spec(vw),
        out_shape=jax.ShapeDtypeStruct((batch, seq, BRANCH_WIDTH), BF16),
        scratch_shapes=scratch,
        compiler_params=_params(2),
        name=name,
    )(*args)


BAND_BLOCKS = C_LEFT_CHUNKS * CHUNK // ATTN_TQ + 1
BAND_WIDTH = BAND_BLOCKS * ATTN_TQ
BAND_PERIOD = BAND_WIDTH + ATTN_TQ


def _band_attn_kernel(q_ref, k_ref, v_ref, w_ref, o_ref, bias_ref, s_ref):
    heads = 2 * PAIRS_PER_STEP
    seq = q_ref.shape[1]
    tq = ATTN_TQ
    lane = lax.broadcasted_iota(jnp.int32, (1, LANES), 1)

    @pl.when(pl.program_id(1) == 0)
    def _():
        r = lax.broadcasted_iota(jnp.int32, (tq, BAND_WIDTH), 0)
        c = lax.broadcasted_iota(jnp.int32, (tq, BAND_WIDTH), 1)
        dchunk = r // CHUNK - (c // CHUNK - C_LEFT_CHUNKS)
        valid = (dchunk >= 0) & (dchunk <= C_LEFT_CHUNKS)
        for head in range(heads):
            table = jnp.broadcast_to(w_ref[head], (tq, BAND_PERIOD))
            toeplitz = pltpu.roll(table, 0, 1, stride=1, stride_axis=0)
            bias_ref[head] = jnp.where(valid, toeplitz[:, :BAND_WIDTH], NEG_INF)

    def unit(t, head):
        k0, k1 = max(0, t - (BAND_BLOCKS - 1)) * tq, (t + 1) * tq
        boff = BAND_WIDTH - (k1 - k0)
        pp, hh = divmod(head, 2)
        cols = slice(pp * LANES, (pp + 1) * LANES)

        def score_blocks():
            q2 = q_ref[0, t * tq:(t + 1) * tq, cols]
            qh = jnp.where(_head_lanes(lane, hh), q2, jnp.zeros_like(q2))
            return [(0, _dot_nt(qh, k_ref[0, k0:k1, cols]) + bias_ref[head, :, boff:])]

        return (score_blocks, lambda a, b: v_ref[0, k0 + a:k0 + b, cols], slice(t * tq, (t + 1) * tq), head)

    _run_units([unit(t, head) for t in _longest_first(seq // tq) for head in range(heads)], s_ref, o_ref)


def _band_attn(q, k, v, w):
    batch, seq, _ = v.shape
    heads = 2 * PAIRS_PER_STEP
    group_spec = pl.BlockSpec((1, seq, PAIRS_PER_STEP * LANES), lambda g, b: (b, 0, g))
    return pl.pallas_call(
        _band_attn_kernel,
        grid=(HEADS // heads, batch),
        in_specs=[group_spec, group_spec, group_spec,
                  pl.BlockSpec((heads, 1, BAND_PERIOD), lambda g, b: (g, 0, 0))],
        out_specs=group_spec,
        out_shape=jax.ShapeDtypeStruct((batch, seq, BRANCH_WIDTH), BF16),
        scratch_shapes=[pltpu.VMEM((heads, ATTN_TQ, BAND_WIDTH), F32),
                        pltpu.VMEM((PIPE_DEPTH, ATTN_TQ, BAND_WIDTH), F32)],
        compiler_params=pltpu.CompilerParams(dimension_semantics=("parallel", "arbitrary"),
                                             vmem_limit_bytes=VMEM_LIMIT),
        name="band_attn",
    )(q, k, v, w)


def _merge_kernel(x_ref, ya_ref, yb_ref, yc_ref, gmix_ref, wg_ref, bg_ref, wb_ref, wo_ref, o_ref):
    x = x_ref[...]
    h = _row_rms(x, gmix_ref[...]).astype(BF16)
    merged = None
    for n, y_ref in enumerate((ya_ref, yb_ref, yc_ref)):
        cols = slice(n * D_MODEL, (n + 1) * D_MODEL)
        gate = jax.nn.sigmoid(_dot_nt(h, wg_ref[cols, :]) + bg_ref[:, cols])
        term = gate * _dot(y_ref[...], wb_ref[n])
        merged = term if merged is None else merged + term
    o_ref[...] = x + _dot(merged.astype(BF16), wo_ref[...])


def _merge(x, ya, yb, yc, gmix, wg, bg, wb, wo):
    t = x.shape[0]
    tm = TOKEN_TILE
    row = lambda width: pl.BlockSpec((tm, width), lambda i: (i, 0))
    consts = [gmix, wg, bg, wb, wo]
    return pl.pallas_call(
        _merge_kernel,
        grid=(t // tm,),
        in_specs=[row(D_MODEL)] + [row(BRANCH_WIDTH)] * 3 + [_const_spec(c) for c in consts],
        out_specs=row(D_MODEL),
        out_shape=jax.ShapeDtypeStruct((t, D_MODEL), F32),
        compiler_params=_params(1),
        name="merge",
    )(x, ya, yb, yc, *_arrays(consts))


FF_CHUNK = 1024


def _ffn_kernel(x_ref, g_ref, w1_ref, w2_ref, o_ref):
    x = x_ref[...]
    h = _row_rms(x, g_ref[...]).astype(BF16)
    acc = x
    for c in range(D_FF // FF_CHUNK):
        a = jnp.maximum(_dot(h, w1_ref[:, c * FF_CHUNK:(c + 1) * FF_CHUNK]), 0.0)
        acc = acc + _dot((a * a).astype(BF16), w2_ref[c * FF_CHUNK:(c + 1) * FF_CHUNK, :])
    o_ref[...] = acc


def _ffn(x, g, w1, w2):
    t = x.shape[0]
    tm = TOKEN_TILE
    row = pl.BlockSpec((tm, D_MODEL), lambda i: (i, 0))
    return pl.pallas_call(
        _ffn_kernel,
        grid=(t // tm,),
        in_specs=[row] + [_const_spec(c) for c in (g, w1, w2)],
        out_specs=row,
        out_shape=jax.ShapeDtypeStruct((t, D_MODEL), F32),
        compiler_params=_params(1),
        name="ffn",
    )(x, *_arrays((g, w1, w2)))


def _group_matrix(groups):
    ids = np.asarray(groups)
    return jnp.asarray((ids[:, None] == ids[None, :]) & (ids[:, None] >= 0), dtype=BF16)


def _band_table(rel_bias):
    rb = rel_bias.astype(F32) * LOG2E
    left = BAND_WIDTH - ATTN_TQ - REL_CLIP
    far = jnp.broadcast_to(rb[..., 2 * REL_CLIP:], rb.shape[:-1] + (BAND_PERIOD,))
    w = jnp.concatenate([far[..., :left], rb[..., ::-1], far[..., left + 2 * REL_CLIP + 1:]], axis=-1)
    return w[..., None, :]


def _rope_tables(seq):
    half = B_ROPE // 2
    inv = ROPE_THETA ** (-jnp.arange(half, dtype=F32) / half)
    ang = jnp.arange(seq).astype(F32)[:, None] * inv[None, :]
    cos, sin = jnp.cos(ang), jnp.sin(ang)
    pad_l = jnp.ones((seq, MISC_KR), F32)
    pad_r = jnp.ones((seq, SLOT - MISC_KR - B_ROPE), F32)
    cos_t = jnp.concatenate([pad_l, cos, cos, pad_r], axis=1)
    sin_t = jnp.concatenate([0 * pad_l, -sin, sin, 0 * pad_r], axis=1)
    return cos_t, sin_t


def kernel(x, norm_mix, w_in, b_forget, b_gate, qk_norm_a, mla_q_norm, mla_kv_norm, w_q_up, w_kv_up,
           qk_norm_b_nope, qk_norm_b_rope, qk_norm_c, rel_bias, w_branch, w_out, norm_ffn, w_ff1, w_ff2):
    batch, seq, _ = x.shape
    depth = w_in.shape[0]
    o = IN_OFFS
    zeros = lambda *s: jnp.zeros(s, F32)

    w_in_t = jnp.swapaxes(w_in, 1, 2)
    w_gate_t = w_in_t[:, o[10]:o[11], :].astype(BF16)

    attn_scale = HEAD_DIM ** -0.5 * LOG2E
    tile8 = lambda g: jnp.tile(g, (1, HEADS))
    hgain = jnp.stack([tile8(qk_norm_a[:, 0]) * attn_scale, tile8(qk_norm_a[:, 1]),
                       tile8(qk_norm_c[:, 0]) * attn_scale, tile8(qk_norm_c[:, 1])], axis=1)

    mla_scale = (B_NOPE + B_ROPE) ** -0.5 * LOG2E
    pad = SLOT - B_NOPE - B_ROPE
    wq_slot = jnp.pad(w_q_up.reshape(depth, B_Q_LORA, HEADS, B_NOPE + B_ROPE),
                      ((0, 0), (0, 0), (0, 0), (0, pad))).reshape(depth, B_Q_LORA, HEADS * SLOT).astype(BF16)
    wkv = w_kv_up.reshape(depth, B_KV_LORA, HEADS, 2 * B_NOPE)
    wk_slot = jnp.pad(wkv[..., :B_NOPE], ((0, 0), (0, 0), (0, 0), (0, SLOT - B_NOPE))
                      ).reshape(depth, B_KV_LORA, HEADS * SLOT).astype(BF16)
    wv = wkv[..., B_NOPE:].reshape(depth, B_KV_LORA, BRANCH_WIDTH).astype(BF16)
    gq_slot = tile8(jnp.concatenate([qk_norm_b_nope[:, 0], qk_norm_b_rope[:, 0], zeros(depth, pad)], axis=1)
                    ) * mla_scale
    gk_slot = tile8(jnp.concatenate([qk_norm_b_nope[:, 1], zeros(depth, SLOT - B_NOPE)], axis=1))
    gkr_slot = jnp.concatenate([zeros(depth, MISC_KR), qk_norm_b_rope[:, 1], zeros(depth, pad)], axis=1)
    slot_groups = np.concatenate([np.zeros(B_NOPE), np.ones(B_ROPE), -np.ones(pad)]).astype(np.int64)
    slot_groups2 = np.concatenate([slot_groups, np.where(slot_groups >= 0, slot_groups + 2, -1)])
    gslot = _group_matrix(slot_groups2)
    invcnt = jnp.asarray(np.tile(np.concatenate([np.full(B_NOPE, 1.0 / B_NOPE), np.full(B_ROPE, 1.0 / B_ROPE),
                                                  np.ones(pad)]), 2)[None, :], dtype=F32)
    cos_t, sin_t = _rope_tables(seq)
    bf_rows = jnp.pad(b_forget.astype(F32), ((0, 0), (0, LANES - HEADS)))
    band_tables = _band_table(rel_bias)

    w_branch_b = w_branch.astype(BF16)
    w_out_b = w_out.astype(BF16)
    w_ff1_b = w_ff1.astype(BF16)
    w_ff2_b = w_ff2.astype(BF16)

    xt = x.reshape(batch * seq, D_MODEL)
    seq3 = lambda a: a.reshape(batch, seq, a.shape[-1])
    for l in range(depth):
        qa, ka, va, qc, kc, vc, qdn, kvdn, misc = _in_proj(
            xt, norm_mix[l][None], _Layer(w_in_t, l, rows=o[10]), _Layer(hgain, l), mla_q_norm[l][None],
            mla_kv_norm[l][None])
        qb, kb, vb, aq, ak = _mla_prep(
            qdn, kvdn, misc, cos_t, sin_t, _Layer(wq_slot, l), _Layer(wk_slot, l), _Layer(wv, l), gslot, invcnt,
            gq_slot[l][None], gk_slot[l][None], gkr_slot[l][None], bf_rows[l][None], seq)
        ya = _prefix_attn(seq3(qa), seq3(ka), seq3(va), seq3(aq), seq3(ak), frame_causal=True, name="fox_attn")
        yb = _prefix_attn(seq3(qb), seq3(kb), seq3(vb), frame_causal=False, name="mla_attn")
        yc = _band_attn(seq3(qc), seq3(kc), seq3(vc), band_tables[l])
        flat = lambda a: a.reshape(batch * seq, BRANCH_WIDTH)
        xt = _merge(xt, flat(ya), flat(yb), flat(yc), norm_mix[l][None], _Layer(w_gate_t, l), b_gate[l][None],
                    _Layer(w_branch_b, l), _Layer(w_out_b, l))
        xt = _ffn(xt, norm_ffn[l][None], _Layer(w_ff1_b, l), _Layer(w_ff2_b, l))
    return xt.reshape(batch, seq, D_MODEL)
```

```python
import functools
from typing import NamedTuple

import numpy as np
import jax
import jax.numpy as jnp
from jax import lax
from jax.experimental import pallas as pl
from jax.experimental.pallas import tpu as pltpu

F32 = jnp.float32
BF16 = jnp.bfloat16

D_MODEL = 1024
DEPTH = 4
CHUNK = 64
EPS = 1e-6
NEG_INF = -1e30
HEADS = 8
HEAD_DIM = 64
B_NOPE = 64
B_ROPE = 32
B_Q_LORA = 384
B_KV_LORA = 256
ROPE_THETA = 10000.0
C_LEFT_CHUNKS = 8
REL_CLIP = 256
N_BRANCH = 3
BRANCH_WIDTH = 512
D_FF = 4 * D_MODEL
IN_SIZES = (512, 512, 512, HEADS, B_Q_LORA, B_KV_LORA, B_ROPE, 512, 512, 512, N_BRANCH * D_MODEL)
IN_OFFS = tuple(int(v) for v in np.cumsum((0,) + IN_SIZES))

LANES = 128
MXU_DIM = 256
SLOT = LANES
TOKEN_TILE = 1024
ATTN_TQ = 256
PAIRS_PER_STEP = 2
PIPE_DEPTH = 2
VMEM_LIMIT = 56 * 1024 * 1024

MISC_KR = 64
LOG2E = float(np.log2(np.e))


def _dot(a, b):
    return jnp.dot(a, b, preferred_element_type=F32)


def _dot_nt(a, b):
    return lax.dot_general(a, b, (((1,), (1,)), ((), ())), preferred_element_type=F32)


def _split2(v):
    hi = v.astype(BF16)
    lo = (v - hi.astype(F32)).astype(BF16)
    return hi, lo


def _group_sumsq(u, g_ref):
    hi, lo = _split2(u * u)
    g = g_ref[...]
    return _dot(hi, g) + _dot(lo, g)


def _row_rms(x, gain):
    ms = jnp.mean(x * x, axis=-1, keepdims=True)
    return x * lax.rsqrt(ms + EPS) * gain


class _Layer(NamedTuple):
    stacked: jax.Array
    index: int
    rows: int = 0


def _const_spec(op):
    if isinstance(op, _Layer):
        tail = op.stacked.shape[1:]
        if op.rows:
            tail = (op.rows,) + tail[1:]
        return pl.BlockSpec((None,) + tail, lambda *_: (op.index,) + (0,) * len(tail),
                            pipeline_mode=pl.Buffered(1))
    return pl.BlockSpec(op.shape, lambda *_: (0,) * op.ndim, pipeline_mode=pl.Buffered(1))


def _arrays(ops):
    return [op.stacked if isinstance(op, _Layer) else op for op in ops]


def _params(n_axes):
    return pltpu.CompilerParams(dimension_semantics=("parallel",) * n_axes,
                                vmem_limit_bytes=VMEM_LIMIT)


def _in_proj_kernel(x_ref, gmix_ref, wt_ref, hgain_ref, gq_ref, gkv_ref,
                    qa_ref, ka_ref, va_ref, qc_ref, kc_ref, vc_ref, qd_ref, kvd_ref, misc_ref):
    h = _row_rms(x_ref[...], gmix_ref[...]).astype(BF16)
    o = IN_OFFS

    first_head = lax.broadcasted_iota(jnp.int32, (1, LANES), 1) < HEAD_DIM

    def project(a, b):
        return _dot_nt(h, wt_ref[a:b, :].astype(BF16))

    def head_normed(a, gain_row):
        u = project(a, a + 512)
        parts = []
        for j in range(512 // LANES):
            ub = u[:, LANES * j:LANES * (j + 1)]
            sq = ub * ub
            s0 = jnp.sum(jnp.where(first_head, sq, 0.0), axis=-1, keepdims=True)
            s1 = jnp.sum(jnp.where(first_head, 0.0, sq), axis=-1, keepdims=True)
            parts.append(ub * lax.rsqrt(jnp.where(first_head, s0, s1) * (1.0 / HEAD_DIM) + EPS))
        return (jnp.concatenate(parts, axis=1) * hgain_ref[gain_row:gain_row + 1, :]).astype(BF16)

    qa_ref[...] = head_normed(o[0], 0)
    ka_ref[...] = head_normed(o[1], 1)
    va_ref[...] = project(o[2], o[3]).astype(BF16)
    qc_ref[...] = head_normed(o[7], 2)
    kc_ref[...] = head_normed(o[8], 3)
    vc_ref[...] = project(o[9], o[10]).astype(BF16)
    zrows = lambda n: jnp.zeros((n, D_MODEL), F32)
    w_tail = jnp.concatenate([wt_ref[o[4]:o[6], :], wt_ref[o[3]:o[4], :], zrows(MISC_KR - HEADS),
                              wt_ref[o[6]:o[7], :], zrows(LANES - MISC_KR - B_ROPE)], axis=0)
    tail = _dot_nt(h, w_tail.astype(BF16))
    qd_ref[...] = _row_rms(tail[:, :B_Q_LORA], gq_ref[...]).astype(BF16)
    kvd_ref[...] = _row_rms(tail[:, B_Q_LORA:B_Q_LORA + B_KV_LORA], gkv_ref[...]).astype(BF16)
    misc_ref[...] = tail[:, B_Q_LORA + B_KV_LORA:]


def _in_proj(x, gmix, w_t, hgain, gq, gkv):
    t = x.shape[0]
    tm = TOKEN_TILE
    row = lambda width: pl.BlockSpec((tm, width), lambda i: (i, 0))
    out_shapes = [jax.ShapeDtypeStruct((t, 512), BF16)] * 6 + [
        jax.ShapeDtypeStruct((t, B_Q_LORA), BF16),
        jax.ShapeDtypeStruct((t, B_KV_LORA), BF16),
        jax.ShapeDtypeStruct((t, LANES), F32)]
    consts = [gmix, w_t, hgain, gq, gkv]
    return pl.pallas_call(
        _in_proj_kernel,
        grid=(t // tm,),
        in_specs=[row(D_MODEL)] + [_const_spec(c) for c in consts],
        out_specs=[row(512)] * 6 + [row(B_Q_LORA), row(B_KV_LORA), row(LANES)],
        out_shape=out_shapes,
        compiler_params=_params(1),
        name="in_proj",
    )(x, *_arrays(consts))


def _split3(v):
    hi = v.astype(BF16)
    r1 = v - hi.astype(F32)
    mid = r1.astype(BF16)
    lo = (r1 - mid.astype(F32)).astype(BF16)
    return hi, mid, lo


AUG_LANES = LANES // HEADS


def _forget_gate_lanes(misc, gate_refs, aq_ref, ak_ref, carry_ref):
    bf_ref, tri_ref, pq_ref, pk_ref, oq_ref, ok_ref = gate_refs
    blk = MXU_DIM
    lane = lax.broadcasted_iota(jnp.int32, (blk, LANES), 1)
    tri = tri_ref[...]
    local = []
    for j in range(misc.shape[0] // blk):
        z = misc[j * blk:(j + 1) * blk, :] + bf_ref[...]
        lf = -(jnp.maximum(-z, 0.0) + jnp.log1p(jnp.exp(-jnp.abs(z))))
        hi, mid, lo = _split3(jnp.where(lane < HEADS, lf, 0.0))
        local.append(_dot(tri, hi) + _dot(tri, mid) + _dot(tri, lo))
    carry = carry_ref[...]
    for j, loc in enumerate(local):
        rows = slice(j * blk, (j + 1) * blk)
        c = loc + carry
        carry = c[blk - 1:blk, :]
        c3 = jnp.concatenate(_split3(c * LOG2E), axis=1)
        aq_ref[rows, :] = (_dot(c3, pq_ref[...]) + oq_ref[...]).astype(BF16)
        ak_ref[rows, :] = (_dot(c3, pk_ref[...]) + ok_ref[...]).astype(BF16)
    carry_ref[...] = carry


def _gate_constants(bf_row):
    pq = np.zeros((3 * LANES, LANES), np.float32)
    pk = np.zeros((3 * LANES, LANES), np.float32)
    oq = np.zeros((1, LANES), np.float32)
    ok = np.zeros((1, LANES), np.float32)
    for h in range(HEADS):
        for t in range(3):
            pq[t * LANES + h, h * AUG_LANES + t] = 1.0
            pk[t * LANES + h, h * AUG_LANES + 3 + t] = -1.0
            oq[0, h * AUG_LANES + 3 + t] = 1.0
            ok[0, h * AUG_LANES + t] = 1.0
    tri = jnp.asarray(np.tril(np.ones((MXU_DIM, MXU_DIM))), dtype=BF16)
    return [bf_row, tri, jnp.asarray(pq, BF16), jnp.asarray(pk, BF16), jnp.asarray(oq), jnp.asarray(ok)]


def _rope_slot(x, cos, sin, lo_half):
    partner = jnp.where(lo_half, pltpu.roll(x, LANES - B_ROPE // 2, 1), pltpu.roll(x, B_ROPE // 2, 1))
    return x * cos + partner * sin


def _mla_prep_kernel(qd_ref, kvd_ref, misc_ref, cos_ref, sin_ref, wq_ref, wk_ref, wv_ref,
                     gslot_ref, invcnt_ref, gq_ref, gk_ref, gkr_ref, *rest, tiles_per_seq):
    gate_refs, (q_ref, k_ref, v_ref, aq_ref, ak_ref, carry_ref) = rest[:6], rest[6:]

    @pl.when(pl.program_id(0) % tiles_per_seq == 0)
    def _():
        carry_ref[...] = jnp.zeros_like(carry_ref)

    _forget_gate_lanes(misc_ref[...], gate_refs, aq_ref, ak_ref, carry_ref)
    tm = qd_ref.shape[0]
    lane = lax.broadcasted_iota(jnp.int32, (tm, SLOT), 1)
    lo_half = lane < MISC_KR + B_ROPE // 2
    in_kr = (lane >= MISC_KR) & (lane < MISC_KR + B_ROPE)
    cos = cos_ref[...]
    sin = sin_ref[...]
    qd = qd_ref[...]
    kvd = kvd_ref[...]

    def slot_normed(u, gain_ref, col):
        ss = _group_sumsq(u, gslot_ref)
        return u * lax.rsqrt(ss * invcnt_ref[...] + EPS) * gain_ref[:, col:col + MXU_DIM]

    m = misc_ref[...]
    ss_kr = jnp.sum(jnp.where(in_kr, m * m, 0.0), axis=-1, keepdims=True)
    kr = _rope_slot(m * lax.rsqrt(ss_kr * (1.0 / B_ROPE) + EPS) * gkr_ref[...], cos, sin, lo_half)

    for j in range(HEADS * SLOT // MXU_DIM):
        col = j * MXU_DIM
        qn = slot_normed(_dot(qd, wq_ref[:, col:col + MXU_DIM]), gq_ref, col)
        kn = slot_normed(_dot(kvd, wk_ref[:, col:col + MXU_DIM]), gk_ref, col)
        for s in range(MXU_DIM // SLOT):
            sl = slice(s * SLOT, (s + 1) * SLOT)
            q_ref[:, col + s * SLOT:col + (s + 1) * SLOT] = _rope_slot(qn[:, sl], cos, sin, lo_half).astype(BF16)
            k_ref[:, col + s * SLOT:col + (s + 1) * SLOT] = (kn[:, sl] + kr).astype(BF16)
    v_ref[...] = _dot(kvd, wv_ref[...]).astype(BF16)


def _mla_prep(qdn, kvdn, misc, cos_t, sin_t, wq, wk, wv, gslot, invcnt, gq, gk, gkr, bf_row, seq):
    t = qdn.shape[0]
    tm = TOKEN_TILE
    per_seq = seq // tm
    row = lambda width: pl.BlockSpec((tm, width), lambda i: (i, 0))
    pos = pl.BlockSpec((tm, SLOT), lambda i: (i % per_seq, 0))
    consts = [wq, wk, wv, gslot, invcnt, gq, gk, gkr] + _gate_constants(bf_row)
    return pl.pallas_call(
        functools.partial(_mla_prep_kernel, tiles_per_seq=per_seq),
        grid=(t // tm,),
        in_specs=[row(B_Q_LORA), row(B_KV_LORA), row(LANES), pos, pos] + [_const_spec(c) for c in consts],
        out_specs=[row(HEADS * SLOT), row(HEADS * SLOT), row(BRANCH_WIDTH), row(LANES), row(LANES)],
        out_shape=[jax.ShapeDtypeStruct((t, HEADS * SLOT), BF16),
                   jax.ShapeDtypeStruct((t, HEADS * SLOT), BF16),
                   jax.ShapeDtypeStruct((t, BRANCH_WIDTH), BF16),
                   jax.ShapeDtypeStruct((t, LANES), BF16),
                   jax.ShapeDtypeStruct((t, LANES), BF16)],
        scratch_shapes=[pltpu.VMEM((1, LANES), F32)],
        compiler_params=pltpu.CompilerParams(dimension_semantics=("arbitrary",), vmem_limit_bytes=VMEM_LIMIT),
        name="mla_prep",
    )(qdn, kvdn, misc, cos_t, sin_t, *_arrays(consts))


def _head_lanes(lane, hh):
    return (lane >= hh * HEAD_DIM) & (lane < (hh + 1) * HEAD_DIM)


def _longest_first(n):
    return list(range(n - 1, -1, -1))


def _run_units(units, s_ref, o_ref):
    depth = s_ref.shape[0]
    lane = lax.broadcasted_iota(jnp.int32, (1, LANES), 1)
    stats = {}

    def scores(n):
        pieces = units[n][0]()
        for c0, blk in pieces:
            s_ref[n % depth, :, c0:c0 + blk.shape[1]] = blk
        stats[n] = (functools.reduce(jnp.maximum, [jnp.max(blk, axis=-1, keepdims=True) for _, blk in pieces]),
                    sum(blk.shape[1] for _, blk in pieces))

    def finish(n):
        _, values, rows, head = units[n]
        m, klen = stats.pop(n)
        p = jnp.exp2(s_ref[n % depth, :, 0:klen] - m)
        l = jnp.sum(p, axis=-1, keepdims=True)
        o = (_dot(p.astype(BF16), values(0, klen)) / l).astype(BF16)
        cols = slice(head // 2 * LANES, (head // 2 + 1) * LANES)
        if head % 2 == 0:
            o_ref[0, rows, cols] = o
        else:
            o_ref[0, rows, cols] = jnp.where(lane < HEAD_DIM, o_ref[0, rows, cols], o)

    for n in range(min(depth, len(units))):
        scores(n)
    for n in range(len(units)):
        finish(n)
        if n + depth < len(units):
            scores(n + depth)


class _Cast(NamedTuple):
    stacked: jax.Array
    index: int


def _cast_specs(casts, steps, step_of):
    in_specs, out_specs, out_shapes = [], [], []
    for c in casts:
        _, rows, cols = c.stacked.shape
        blk = rows // steps
        in_specs.append(pl.BlockSpec((None, blk, cols), lambda *ids, c=c: (c.index, step_of(*ids), 0)))
        out_specs.append(pl.BlockSpec((blk, cols), lambda *ids: (step_of(*ids), 0)))
        out_shapes.append(jax.ShapeDtypeStruct((rows, cols), BF16))
    return in_specs, out_specs, out_shapes


def _convert_blocks(src_refs, dst_refs):
    for src, dst in zip(src_refs, dst_refs):
        dst[...] = src[...].astype(BF16)


def _prefix_attn_kernel(*refs, frame_causal, decay, n_cast):
    heads = 2 * PAIRS_PER_STEP
    n_in = 5 if decay else 3
    _convert_blocks(refs[n_in:n_in + n_cast], refs[n_in + n_cast + 1:n_in + 2 * n_cast + 1])
    refs = refs[:n_in] + (refs[n_in + n_cast],) + refs[n_in + 2 * n_cast + 1:]
    if decay:
        q_ref, k_ref, v_ref, aq_ref, ak_ref, o_ref, s_ref, kx_ref = refs
        for pp in range(PAIRS_PER_STEP):
            kx_ref[pp, :, 0:LANES] = k_ref[0, :, pp * LANES:(pp + 1) * LANES]
            kx_ref[pp, :, LANES:2 * LANES] = ak_ref[0]
    else:
        q_ref, k_ref, v_ref, o_ref, s_ref = refs
    seq = q_ref.shape[1]
    tq = ATTN_TQ
    lane = lax.broadcasted_iota(jnp.int32, (1, LANES), 1)
    row = lax.broadcasted_iota(jnp.int32, (tq, tq), 0)
    col = lax.broadcasted_iota(jnp.int32, (tq, tq), 1)
    if frame_causal:
        diag_ok = col <= row
    else:
        diag_ok = (col // CHUNK) <= (row // CHUNK)

    def unit(i, head):
        q0, q1 = i * tq, (i + 1) * tq
        pp, hh = divmod(head, 2)

        def score_blocks():
            if decay:
                q2 = q_ref[0, q0:q1, pp * LANES:(pp + 1) * LANES]
                a2 = aq_ref[0, q0:q1, :]
                aug0 = (heads * pl.program_id(1) + head) * AUG_LANES
                qh = jnp.concatenate(
                    [jnp.where(_head_lanes(lane, hh), q2, jnp.zeros_like(q2)),
                     jnp.where((lane >= aug0) & (lane < aug0 + AUG_LANES), a2, jnp.zeros_like(a2))], axis=1)
                k_of = lambda a, b: kx_ref[pp, a:b, :]
            else:
                qh = q_ref[0, q0:q1, head * SLOT:(head + 1) * SLOT]
                k_of = lambda a, b: k_ref[0, a:b, head * SLOT:(head + 1) * SLOT]
            pieces = [(0, _dot_nt(qh, k_of(0, q0)))] if i else []
            return pieces + [(q0, jnp.where(diag_ok, _dot_nt(qh, k_of(q0, q1)), NEG_INF))]

        return (score_blocks, lambda a, b: v_ref[0, a:b, pp * LANES:(pp + 1) * LANES], slice(q0, q1), head)

    _run_units([unit(i, head) for i in _longest_first(seq // tq) for head in range(heads)], s_ref, o_ref)


def _prefix_attn(q, k, v, aq=None, ak=None, *, frame_causal, name, casts=()):
    batch, seq, _ = v.shape
    groups = HEADS // (2 * PAIRS_PER_STEP)
    decay = aq is not None
    vw = PAIRS_PER_STEP * LANES
    qk_w = vw if decay else 2 * PAIRS_PER_STEP * SLOT
    group_spec = lambda w: pl.BlockSpec((1, seq, w), lambda b, g: (b, 0, g))
    in_specs = [group_spec(qk_w), group_spec(qk_w), group_spec(vw)]
    args = [q, k, v]
    scratch = [pltpu.VMEM((PIPE_DEPTH, ATTN_TQ, seq), F32)]
    if decay:
        aug_spec = pl.BlockSpec((1, seq, LANES), lambda b, g: (b, 0, 0))
        in_specs += [aug_spec, aug_spec]
        args += [aq, ak]
        scratch += [pltpu.VMEM((PAIRS_PER_STEP, seq, 2 * LANES), BF16)]
    cast_in, cast_out, cast_shapes = _cast_specs(casts, batch * groups, lambda b, g: b * groups + g)
    return pl.pallas_call(
        functools.partial(_prefix_attn_kernel, frame_causal=frame_causal, decay=decay, n_cast=len(casts)),
        grid=(batch, groups),
        in_specs=in_specs + cast_in,
        out_specs=[group_spec(vw)] + cast_out,
        out_shape=[jax.ShapeDtypeStruct((batch, seq, BRANCH_WIDTH), BF16)] + cast_shapes,
        scratch_shapes=scratch,
        compiler_params=_params(2),
        name=name,
    )(*args, *[c.stacked for c in casts])


BAND_BLOCKS = C_LEFT_CHUNKS * CHUNK // ATTN_TQ + 1
BAND_WIDTH = BAND_BLOCKS * ATTN_TQ
BAND_PERIOD = BAND_WIDTH + ATTN_TQ


def _band_attn_kernel(q_ref, k_ref, v_ref, w_ref, *rest, n_cast):
    _convert_blocks(rest[:n_cast], rest[n_cast + 1:2 * n_cast + 1])
    o_ref, (bias_ref, s_ref) = rest[n_cast], rest[2 * n_cast + 1:]
    heads = 2 * PAIRS_PER_STEP
    seq = q_ref.shape[1]
    tq = ATTN_TQ
    lane = lax.broadcasted_iota(jnp.int32, (1, LANES), 1)

    @pl.when(pl.program_id(1) == 0)
    def _():
        r = lax.broadcasted_iota(jnp.int32, (tq, BAND_WIDTH), 0)
        c = lax.broadcasted_iota(jnp.int32, (tq, BAND_WIDTH), 1)
        dchunk = r // CHUNK - (c // CHUNK - C_LEFT_CHUNKS)
        valid = (dchunk >= 0) & (dchunk <= C_LEFT_CHUNKS)
        for head in range(heads):
            table = jnp.broadcast_to(w_ref[head], (tq, BAND_PERIOD))
            toeplitz = pltpu.roll(table, 0, 1, stride=1, stride_axis=0)
            bias_ref[head] = jnp.where(valid, toeplitz[:, :BAND_WIDTH], NEG_INF)

    def unit(t, head):
        k0, k1 = max(0, t - (BAND_BLOCKS - 1)) * tq, (t + 1) * tq
        boff = BAND_WIDTH - (k1 - k0)
        pp, hh = divmod(head, 2)
        cols = slice(pp * LANES, (pp + 1) * LANES)

        def score_blocks():
            q2 = q_ref[0, t * tq:(t + 1) * tq, cols]
            qh = jnp.where(_head_lanes(lane, hh), q2, jnp.zeros_like(q2))
            return [(0, _dot_nt(qh, k_ref[0, k0:k1, cols]) + bias_ref[head, :, boff:])]

        return (score_blocks, lambda a, b: v_ref[0, k0 + a:k0 + b, cols], slice(t * tq, (t + 1) * tq), head)

    _run_units([unit(t, head) for t in _longest_first(seq // tq) for head in range(heads)], s_ref, o_ref)


def _band_attn(q, k, v, w, casts=()):
    batch, seq, _ = v.shape
    heads = 2 * PAIRS_PER_STEP
    groups = HEADS // heads
    group_spec = pl.BlockSpec((1, seq, PAIRS_PER_STEP * LANES), lambda g, b: (b, 0, g))
    cast_in, cast_out, cast_shapes = _cast_specs(casts, batch * groups, lambda g, b: g * batch + b)
    return pl.pallas_call(
        functools.partial(_band_attn_kernel, n_cast=len(casts)),
        grid=(groups, batch),
        in_specs=[group_spec, group_spec, group_spec,
                  pl.BlockSpec((heads, 1, BAND_PERIOD), lambda g, b: (g, 0, 0))] + cast_in,
        out_specs=[group_spec] + cast_out,
        out_shape=[jax.ShapeDtypeStruct((batch, seq, BRANCH_WIDTH), BF16)] + cast_shapes,
        scratch_shapes=[pltpu.VMEM((heads, ATTN_TQ, BAND_WIDTH), F32),
                        pltpu.VMEM((PIPE_DEPTH, ATTN_TQ, BAND_WIDTH), F32)],
        compiler_params=pltpu.CompilerParams(dimension_semantics=("parallel", "arbitrary"),
                                             vmem_limit_bytes=VMEM_LIMIT),
        name="band_attn",
    )(q, k, v, w, *[c.stacked for c in casts])


def _merge_kernel(x_ref, ya_ref, yb_ref, yc_ref, gmix_ref, wg_ref, bg_ref, wb_ref, wo_ref, o_ref):
    x = x_ref[...]
    h = _row_rms(x, gmix_ref[...]).astype(BF16)
    merged = None
    for n, y_ref in enumerate((ya_ref, yb_ref, yc_ref)):
        cols = slice(n * D_MODEL, (n + 1) * D_MODEL)
        gate = jax.nn.sigmoid(_dot_nt(h, wg_ref[cols, :]) + bg_ref[:, cols])
        term = gate * _dot(y_ref[...], wb_ref[n])
        merged = term if merged is None else merged + term
    o_ref[...] = x + _dot(merged.astype(BF16), wo_ref[...])


def _merge(x, ya, yb, yc, gmix, wg, bg, wb, wo):
    t = x.shape[0]
    tm = TOKEN_TILE
    row = lambda width: pl.BlockSpec((tm, width), lambda i: (i, 0))
    consts = [gmix, wg, bg, wb, wo]
    return pl.pallas_call(
        _merge_kernel,
        grid=(t // tm,),
        in_specs=[row(D_MODEL)] + [row(BRANCH_WIDTH)] * 3 + [_const_spec(c) for c in consts],
        out_specs=row(D_MODEL),
        out_shape=jax.ShapeDtypeStruct((t, D_MODEL), F32),
        compiler_params=_params(1),
        name="merge",
    )(x, ya, yb, yc, *_arrays(consts))


FF_CHUNK = 1024


def _ffn_kernel(x_ref, g_ref, w1_ref, w2_ref, o_ref):
    x = x_ref[...]
    h = _row_rms(x, g_ref[...]).astype(BF16)
    acc = x
    for c in range(D_FF // FF_CHUNK):
        a = jnp.maximum(_dot(h, w1_ref[:, c * FF_CHUNK:(c + 1) * FF_CHUNK]), 0.0)
        acc = acc + _dot((a * a).astype(BF16), w2_ref[c * FF_CHUNK:(c + 1) * FF_CHUNK, :])
    o_ref[...] = acc


def _ffn(x, g, w1, w2):
    t = x.shape[0]
    tm = TOKEN_TILE
    row = pl.BlockSpec((tm, D_MODEL), lambda i: (i, 0))
    return pl.pallas_call(
        _ffn_kernel,
        grid=(t // tm,),
        in_specs=[row] + [_const_spec(c) for c in (g, w1, w2)],
        out_specs=row,
        out_shape=jax.ShapeDtypeStruct((t, D_MODEL), F32),
        compiler_params=_params(1),
        name="ffn",
    )(x, *_arrays((g, w1, w2)))


def _group_matrix(groups):
    ids = np.asarray(groups)
    return jnp.asarray((ids[:, None] == ids[None, :]) & (ids[:, None] >= 0), dtype=BF16)


def _band_table(rel_bias):
    rb = rel_bias.astype(F32) * LOG2E
    left = BAND_WIDTH - ATTN_TQ - REL_CLIP
    far = jnp.broadcast_to(rb[..., 2 * REL_CLIP:], rb.shape[:-1] + (BAND_PERIOD,))
    w = jnp.concatenate([far[..., :left], rb[..., ::-1], far[..., left + 2 * REL_CLIP + 1:]], axis=-1)
    return w[..., None, :]


def _rope_tables(seq):
    half = B_ROPE // 2
    inv = ROPE_THETA ** (-jnp.arange(half, dtype=F32) / half)
    ang = jnp.arange(seq).astype(F32)[:, None] * inv[None, :]
    cos, sin = jnp.cos(ang), jnp.sin(ang)
    pad_l = jnp.ones((seq, MISC_KR), F32)
    pad_r = jnp.ones((seq, SLOT - MISC_KR - B_ROPE), F32)
    cos_t = jnp.concatenate([pad_l, cos, cos, pad_r], axis=1)
    sin_t = jnp.concatenate([0 * pad_l, -sin, sin, 0 * pad_r], axis=1)
    return cos_t, sin_t


def kernel(x, norm_mix, w_in, b_forget, b_gate, qk_norm_a, mla_q_norm, mla_kv_norm, w_q_up, w_kv_up,
           qk_norm_b_nope, qk_norm_b_rope, qk_norm_c, rel_bias, w_branch, w_out, norm_ffn, w_ff1, w_ff2):
    batch, seq, _ = x.shape
    depth = w_in.shape[0]
    o = IN_OFFS
    zeros = lambda *s: jnp.zeros(s, F32)

    w_in_t = jnp.swapaxes(w_in, 1, 2)
    w_gate_t = w_in_t[:, o[10]:o[11], :].astype(BF16)

    attn_scale = HEAD_DIM ** -0.5 * LOG2E
    tile8 = lambda g: jnp.tile(g, (1, HEADS))
    hgain = jnp.stack([tile8(qk_norm_a[:, 0]) * attn_scale, tile8(qk_norm_a[:, 1]),
                       tile8(qk_norm_c[:, 0]) * attn_scale, tile8(qk_norm_c[:, 1])], axis=1)

    mla_scale = (B_NOPE + B_ROPE) ** -0.5 * LOG2E
    pad = SLOT - B_NOPE - B_ROPE
    wq_slot = jnp.pad(w_q_up.reshape(depth, B_Q_LORA, HEADS, B_NOPE + B_ROPE),
                      ((0, 0), (0, 0), (0, 0), (0, pad))).reshape(depth, B_Q_LORA, HEADS * SLOT).astype(BF16)
    wkv = w_kv_up.reshape(depth, B_KV_LORA, HEADS, 2 * B_NOPE)
    wk_slot = jnp.pad(wkv[..., :B_NOPE], ((0, 0), (0, 0), (0, 0), (0, SLOT - B_NOPE))
                      ).reshape(depth, B_KV_LORA, HEADS * SLOT).astype(BF16)
    wv = wkv[..., B_NOPE:].reshape(depth, B_KV_LORA, BRANCH_WIDTH).astype(BF16)
    gq_slot = tile8(jnp.concatenate([qk_norm_b_nope[:, 0], qk_norm_b_rope[:, 0], zeros(depth, pad)], axis=1)
                    ) * mla_scale
    gk_slot = tile8(jnp.concatenate([qk_norm_b_nope[:, 1], zeros(depth, SLOT - B_NOPE)], axis=1))
    gkr_slot = jnp.concatenate([zeros(depth, MISC_KR), qk_norm_b_rope[:, 1], zeros(depth, pad)], axis=1)
    slot_groups = np.concatenate([np.zeros(B_NOPE), np.ones(B_ROPE), -np.ones(pad)]).astype(np.int64)
    slot_groups2 = np.concatenate([slot_groups, np.where(slot_groups >= 0, slot_groups + 2, -1)])
    gslot = _group_matrix(slot_groups2)
    invcnt = jnp.asarray(np.tile(np.concatenate([np.full(B_NOPE, 1.0 / B_NOPE), np.full(B_ROPE, 1.0 / B_ROPE),
                                                  np.ones(pad)]), 2)[None, :], dtype=F32)
    cos_t, sin_t = _rope_tables(seq)
    bf_rows = jnp.pad(b_forget.astype(F32), ((0, 0), (0, LANES - HEADS)))
    band_tables = _band_table(rel_bias)

    w_branch_rows = w_branch.reshape(depth, N_BRANCH * BRANCH_WIDTH, D_MODEL)

    xt = x.reshape(batch * seq, D_MODEL)
    seq3 = lambda a: a.reshape(batch, seq, a.shape[-1])
    for l in range(depth):
        qa, ka, va, qc, kc, vc, qdn, kvdn, misc = _in_proj(
            xt, norm_mix[l][None], _Layer(w_in_t, l, rows=o[10]), _Layer(hgain, l), mla_q_norm[l][None],
            mla_kv_norm[l][None])
        qb, kb, vb, aq, ak = _mla_prep(
            qdn, kvdn, misc, cos_t, sin_t, _Layer(wq_slot, l), _Layer(wk_slot, l), _Layer(wv, l), gslot, invcnt,
            gq_slot[l][None], gk_slot[l][None], gkr_slot[l][None], bf_rows[l][None], seq)
        (ya,) = _prefix_attn(seq3(qa), seq3(ka), seq3(va), seq3(aq), seq3(ak), frame_causal=True, name="fox_attn")
        yb, w_out_b, w_branch_b = _prefix_attn(seq3(qb), seq3(kb), seq3(vb), frame_causal=False, name="mla_attn",
                                               casts=(_Cast(w_out, l), _Cast(w_branch_rows, l)))
        yc, w_ff1_b, w_ff2_b = _band_attn(seq3(qc), seq3(kc), seq3(vc), band_tables[l],
                                          casts=(_Cast(w_ff1, l), _Cast(w_ff2, l)))
        flat = lambda a: a.reshape(batch * seq, BRANCH_WIDTH)
        xt = _merge(xt, flat(ya), flat(yb), flat(yc), norm_mix[l][None], _Layer(w_gate_t, l), b_gate[l][None],
                    w_branch_b.reshape(N_BRANCH, BRANCH_WIDTH, D_MODEL), w_out_b)
        xt = _ffn(xt, norm_ffn[l][None], w_ff1_b, w_ff2_b)
    return xt.reshape(batch, seq, D_MODEL)
```

```python
import functools
from typing import NamedTuple

import numpy as np
import jax
import jax.numpy as jnp
from jax import lax
from jax.experimental import pallas as pl
from jax.experimental.pallas import tpu as pltpu

F32 = jnp.float32
BF16 = jnp.bfloat16

D_MODEL = 1024
DEPTH = 4
CHUNK = 64
EPS = 1e-6
NEG_INF = -1e30
HEADS = 8
HEAD_DIM = 64
B_NOPE = 64
B_ROPE = 32
B_Q_LORA = 384
B_KV_LORA = 256
ROPE_THETA = 10000.0
C_LEFT_CHUNKS = 8
REL_CLIP = 256
N_BRANCH = 3
BRANCH_WIDTH = 512
D_FF = 4 * D_MODEL
IN_SIZES = (512, 512, 512, HEADS, B_Q_LORA, B_KV_LORA, B_ROPE, 512, 512, 512, N_BRANCH * D_MODEL)
IN_OFFS = tuple(int(v) for v in np.cumsum((0,) + IN_SIZES))

LANES = 128
MXU_DIM = 256
SLOT = LANES
TOKEN_TILE = 1024
ATTN_TQ = 256
PAIRS_PER_STEP = 2
PIPE_DEPTH = 2
VMEM_LIMIT = 56 * 1024 * 1024

MISC_KR = 64
LOG2E = float(np.log2(np.e))


def _dot(a, b):
    return jnp.dot(a, b, preferred_element_type=F32)


def _dot_nt(a, b):
    return lax.dot_general(a, b, (((1,), (1,)), ((), ())), preferred_element_type=F32)


def _split2(v):
    hi = v.astype(BF16)
    lo = (v - hi.astype(F32)).astype(BF16)
    return hi, lo


def _group_sumsq(u, g_ref):
    hi, lo = _split2(u * u)
    g = g_ref[...]
    return _dot(hi, g) + _dot(lo, g)


def _row_rms(x, gain):
    ms = jnp.mean(x * x, axis=-1, keepdims=True)
    return x * lax.rsqrt(ms + EPS) * gain


class _Layer(NamedTuple):
    stacked: jax.Array
    index: int
    rows: int = 0


def _const_spec(op):
    if isinstance(op, _Layer):
        tail = op.stacked.shape[1:]
        if op.rows:
            tail = (op.rows,) + tail[1:]
        return pl.BlockSpec((None,) + tail, lambda *_: (op.index,) + (0,) * len(tail),
                            pipeline_mode=pl.Buffered(1))
    return pl.BlockSpec(op.shape, lambda *_: (0,) * op.ndim, pipeline_mode=pl.Buffered(1))


def _arrays(ops):
    return [op.stacked if isinstance(op, _Layer) else op for op in ops]


def _params(n_axes):
    return pltpu.CompilerParams(dimension_semantics=("parallel",) * n_axes,
                                vmem_limit_bytes=VMEM_LIMIT)


def _in_proj_kernel(x_ref, gmix_ref, wt_ref, hgain_ref, gq_ref, gkv_ref,
                    qa_ref, ka_ref, va_ref, qc_ref, kc_ref, vc_ref, qd_ref, kvd_ref, misc_ref):
    h = _row_rms(x_ref[...], gmix_ref[...]).astype(BF16)
    o = IN_OFFS

    first_head = lax.broadcasted_iota(jnp.int32, (1, LANES), 1) < HEAD_DIM

    def project(a, b):
        return _dot_nt(h, wt_ref[a:b, :].astype(BF16))

    def head_normed(a, gain_row):
        u = project(a, a + 512)
        parts = []
        for j in range(512 // LANES):
            ub = u[:, LANES * j:LANES * (j + 1)]
            sq = ub * ub
            s0 = jnp.sum(jnp.where(first_head, sq, 0.0), axis=-1, keepdims=True)
            s1 = jnp.sum(jnp.where(first_head, 0.0, sq), axis=-1, keepdims=True)
            parts.append(ub * lax.rsqrt(jnp.where(first_head, s0, s1) * (1.0 / HEAD_DIM) + EPS))
        return (jnp.concatenate(parts, axis=1) * hgain_ref[gain_row:gain_row + 1, :]).astype(BF16)

    qa_ref[...] = head_normed(o[0], 0)
    ka_ref[...] = head_normed(o[1], 1)
    va_ref[...] = project(o[2], o[3]).astype(BF16)
    qc_ref[...] = head_normed(o[7], 2)
    kc_ref[...] = head_normed(o[8], 3)
    vc_ref[...] = project(o[9], o[10]).astype(BF16)
    zrows = lambda n: jnp.zeros((n, D_MODEL), F32)
    w_tail = jnp.concatenate([wt_ref[o[4]:o[6], :], wt_ref[o[3]:o[4], :], zrows(MISC_KR - HEADS),
                              wt_ref[o[6]:o[7], :], zrows(LANES - MISC_KR - B_ROPE)], axis=0)
    tail = _dot_nt(h, w_tail.astype(BF16))
    qd_ref[...] = _row_rms(tail[:, :B_Q_LORA], gq_ref[...]).astype(BF16)
    kvd_ref[...] = _row_rms(tail[:, B_Q_LORA:B_Q_LORA + B_KV_LORA], gkv_ref[...]).astype(BF16)
    misc_ref[...] = tail[:, B_Q_LORA + B_KV_LORA:]


def _in_proj(x, gmix, w_t, hgain, gq, gkv):
    t = x.shape[0]
    tm = TOKEN_TILE
    row = lambda width: pl.BlockSpec((tm, width), lambda i: (i, 0))
    out_shapes = [jax.ShapeDtypeStruct((t, 512), BF16)] * 6 + [
        jax.ShapeDtypeStruct((t, B_Q_LORA), BF16),
        jax.ShapeDtypeStruct((t, B_KV_LORA), BF16),
        jax.ShapeDtypeStruct((t, LANES), F32)]
    consts = [gmix, w_t, hgain, gq, gkv]
    return pl.pallas_call(
        _in_proj_kernel,
        grid=(t // tm,),
        in_specs=[row(D_MODEL)] + [_const_spec(c) for c in consts],
        out_specs=[row(512)] * 6 + [row(B_Q_LORA), row(B_KV_LORA), row(LANES)],
        out_shape=out_shapes,
        compiler_params=_params(1),
        name="in_proj",
    )(x, *_arrays(consts))


def _split3(v):
    hi = v.astype(BF16)
    r1 = v - hi.astype(F32)
    mid = r1.astype(BF16)
    lo = (r1 - mid.astype(F32)).astype(BF16)
    return hi, mid, lo


AUG_LANES = LANES // HEADS


def _forget_gate_lanes(misc, gate_refs, aq_ref, ak_ref, carry_ref):
    bf_ref, tri_ref, pq_ref, pk_ref, oq_ref, ok_ref = gate_refs
    blk = MXU_DIM
    lane = lax.broadcasted_iota(jnp.int32, (blk, LANES), 1)
    tri = tri_ref[...]
    local = []
    for j in range(misc.shape[0] // blk):
        z = misc[j * blk:(j + 1) * blk, :] + bf_ref[...]
        lf = -(jnp.maximum(-z, 0.0) + jnp.log1p(jnp.exp(-jnp.abs(z))))
        hi, mid, lo = _split3(jnp.where(lane < HEADS, lf, 0.0))
        local.append(_dot(tri, hi) + _dot(tri, mid) + _dot(tri, lo))
    carry = carry_ref[...]
    for j, loc in enumerate(local):
        rows = slice(j * blk, (j + 1) * blk)
        c = loc + carry
        carry = c[blk - 1:blk, :]
        c3 = jnp.concatenate(_split3(c * LOG2E), axis=1)
        aq_ref[rows, :] = (_dot(c3, pq_ref[...]) + oq_ref[...]).astype(BF16)
        ak_ref[rows, :] = (_dot(c3, pk_ref[...]) + ok_ref[...]).astype(BF16)
    carry_ref[...] = carry


def _gate_constants(bf_row):
    pq = np.zeros((3 * LANES, LANES), np.float32)
    pk = np.zeros((3 * LANES, LANES), np.float32)
    oq = np.zeros((1, LANES), np.float32)
    ok = np.zeros((1, LANES), np.float32)
    for h in range(HEADS):
        for t in range(3):
            pq[t * LANES + h, h * AUG_LANES + t] = 1.0
            pk[t * LANES + h, h * AUG_LANES + 3 + t] = -1.0
            oq[0, h * AUG_LANES + 3 + t] = 1.0
            ok[0, h * AUG_LANES + t] = 1.0
    tri = jnp.asarray(np.tril(np.ones((MXU_DIM, MXU_DIM))), dtype=BF16)
    return [bf_row, tri, jnp.asarray(pq, BF16), jnp.asarray(pk, BF16), jnp.asarray(oq), jnp.asarray(ok)]


def _rope_slot(x, cos, sin, lo_half):
    partner = jnp.where(lo_half, pltpu.roll(x, LANES - B_ROPE // 2, 1), pltpu.roll(x, B_ROPE // 2, 1))
    return x * cos + partner * sin


def _mla_prep_kernel(qd_ref, kvd_ref, misc_ref, cos_ref, sin_ref, wq_ref, wk_ref, wv_ref,
                     gslot_ref, invcnt_ref, gq_ref, gk_ref, gkr_ref, *rest, tiles_per_seq):
    gate_refs, (q_ref, k_ref, v_ref, aq_ref, ak_ref, carry_ref) = rest[:6], rest[6:]

    @pl.when(pl.program_id(0) % tiles_per_seq == 0)
    def _():
        carry_ref[...] = jnp.zeros_like(carry_ref)

    _forget_gate_lanes(misc_ref[...], gate_refs, aq_ref, ak_ref, carry_ref)
    tm = qd_ref.shape[0]
    lane = lax.broadcasted_iota(jnp.int32, (tm, SLOT), 1)
    lo_half = lane < MISC_KR + B_ROPE // 2
    in_kr = (lane >= MISC_KR) & (lane < MISC_KR + B_ROPE)
    cos = cos_ref[...]
    sin = sin_ref[...]
    qd = qd_ref[...]
    kvd = kvd_ref[...]

    def slot_normed(u, gain_ref, col):
        ss = _group_sumsq(u, gslot_ref)
        return u * lax.rsqrt(ss * invcnt_ref[...] + EPS) * gain_ref[:, col:col + MXU_DIM]

    m = misc_ref[...]
    ss_kr = jnp.sum(jnp.where(in_kr, m * m, 0.0), axis=-1, keepdims=True)
    kr = _rope_slot(m * lax.rsqrt(ss_kr * (1.0 / B_ROPE) + EPS) * gkr_ref[...], cos, sin, lo_half)

    for j in range(HEADS * SLOT // MXU_DIM):
        col = j * MXU_DIM
        qn = slot_normed(_dot(qd, wq_ref[:, col:col + MXU_DIM]), gq_ref, col)
        kn = slot_normed(_dot(kvd, wk_ref[:, col:col + MXU_DIM]), gk_ref, col)
        for s in range(MXU_DIM // SLOT):
            sl = slice(s * SLOT, (s + 1) * SLOT)
            q_ref[:, col + s * SLOT:col + (s + 1) * SLOT] = _rope_slot(qn[:, sl], cos, sin, lo_half).astype(BF16)
            k_ref[:, col + s * SLOT:col + (s + 1) * SLOT] = (kn[:, sl] + kr).astype(BF16)
    v_ref[...] = _dot(kvd, wv_ref[...]).astype(BF16)


def _mla_prep(qdn, kvdn, misc, cos_t, sin_t, wq, wk, wv, gslot, invcnt, gq, gk, gkr, bf_row, seq):
    t = qdn.shape[0]
    tm = TOKEN_TILE
    per_seq = seq // tm
    row = lambda width: pl.BlockSpec((tm, width), lambda i: (i, 0))
    pos = pl.BlockSpec((tm, SLOT), lambda i: (i % per_seq, 0))
    consts = [wq, wk, wv, gslot, invcnt, gq, gk, gkr] + _gate_constants(bf_row)
    return pl.pallas_call(
        functools.partial(_mla_prep_kernel, tiles_per_seq=per_seq),
        grid=(t // tm,),
        in_specs=[row(B_Q_LORA), row(B_KV_LORA), row(LANES), pos, pos] + [_const_spec(c) for c in consts],
        out_specs=[row(HEADS * SLOT), row(HEADS * SLOT), row(BRANCH_WIDTH), row(LANES), row(LANES)],
        out_shape=[jax.ShapeDtypeStruct((t, HEADS * SLOT), BF16),
                   jax.ShapeDtypeStruct((t, HEADS * SLOT), BF16),
                   jax.ShapeDtypeStruct((t, BRANCH_WIDTH), BF16),
                   jax.ShapeDtypeStruct((t, LANES), BF16),
                   jax.ShapeDtypeStruct((t, LANES), BF16)],
        scratch_shapes=[pltpu.VMEM((1, LANES), F32)],
        compiler_params=pltpu.CompilerParams(dimension_semantics=("arbitrary",), vmem_limit_bytes=VMEM_LIMIT),
        name="mla_prep",
    )(qdn, kvdn, misc, cos_t, sin_t, *_arrays(consts))


def _head_lanes(lane, hh):
    return (lane >= hh * HEAD_DIM) & (lane < (hh + 1) * HEAD_DIM)


def _longest_first(n):
    return list(range(n - 1, -1, -1))


def _run_units(units, s_ref, o_ref):
    depth = s_ref.shape[0]
    lane = lax.broadcasted_iota(jnp.int32, (1, LANES), 1)
    stats = {}

    def scores(n):
        pieces = units[n][0]()
        for c0, blk in pieces:
            s_ref[n % depth, :, c0:c0 + blk.shape[1]] = blk
        stats[n] = (functools.reduce(jnp.maximum, [jnp.max(blk, axis=-1, keepdims=True) for _, blk in pieces]),
                    sum(blk.shape[1] for _, blk in pieces))

    def finish(n):
        _, values, rows, head = units[n]
        m, klen = stats.pop(n)
        p = jnp.exp2(s_ref[n % depth, :, 0:klen] - m)
        l = jnp.sum(p, axis=-1, keepdims=True)
        o = (_dot(p.astype(BF16), values(0, klen)) / l).astype(BF16)
        cols = slice(head // 2 * LANES, (head // 2 + 1) * LANES)
        if head % 2 == 0:
            o_ref[0, rows, cols] = o
        else:
            o_ref[0, rows, cols] = jnp.where(lane < HEAD_DIM, o_ref[0, rows, cols], o)

    for n in range(min(depth, len(units))):
        scores(n)
    for n in range(len(units)):
        finish(n)
        if n + depth < len(units):
            scores(n + depth)


class _Cast(NamedTuple):
    stacked: jax.Array
    index: int
    row0: int = 0
    rows: int = 0


def _cast_specs(casts, steps, step_of):
    in_specs, out_specs, out_shapes = [], [], []
    for c in casts:
        _, rows, cols = c.stacked.shape
        rows = c.rows or rows
        blk = rows // steps
        if c.row0:
            in_specs.append(pl.BlockSpec((pl.Element(1), pl.Element(blk), pl.Element(cols)),
                                         lambda *ids, c=c, blk=blk: (
                                             c.index, pl.multiple_of(c.row0 + step_of(*ids) * blk, 8), 0)))
        else:
            in_specs.append(pl.BlockSpec((None, blk, cols), lambda *ids, c=c: (c.index, step_of(*ids), 0)))
        out_specs.append(pl.BlockSpec((blk, cols), lambda *ids: (step_of(*ids), 0)))
        out_shapes.append(jax.ShapeDtypeStruct((rows, cols), BF16))
    return in_specs, out_specs, out_shapes


def _convert_blocks(src_refs, dst_refs):
    for src, dst in zip(src_refs, dst_refs):
        dst[...] = src[...].reshape(dst.shape).astype(BF16)


def _prefix_attn_kernel(*refs, frame_causal, decay, n_cast):
    heads = 2 * PAIRS_PER_STEP
    n_in = 5 if decay else 3
    _convert_blocks(refs[n_in:n_in + n_cast], refs[n_in + n_cast + 1:n_in + 2 * n_cast + 1])
    refs = refs[:n_in] + (refs[n_in + n_cast],) + refs[n_in + 2 * n_cast + 1:]
    if decay:
        q_ref, k_ref, v_ref, aq_ref, ak_ref, o_ref, s_ref, kx_ref = refs
        for pp in range(PAIRS_PER_STEP):
            kx_ref[pp, :, 0:LANES] = k_ref[0, :, pp * LANES:(pp + 1) * LANES]
            kx_ref[pp, :, LANES:2 * LANES] = ak_ref[0]
    else:
        q_ref, k_ref, v_ref, o_ref, s_ref = refs
    seq = q_ref.shape[1]
    tq = ATTN_TQ
    lane = lax.broadcasted_iota(jnp.int32, (1, LANES), 1)
    row = lax.broadcasted_iota(jnp.int32, (tq, tq), 0)
    col = lax.broadcasted_iota(jnp.int32, (tq, tq), 1)
    if frame_causal:
        diag_ok = col <= row
    else:
        diag_ok = (col // CHUNK) <= (row // CHUNK)

    def unit(i, head):
        q0, q1 = i * tq, (i + 1) * tq
        pp, hh = divmod(head, 2)

        def score_blocks():
            if decay:
                q2 = q_ref[0, q0:q1, pp * LANES:(pp + 1) * LANES]
                a2 = aq_ref[0, q0:q1, :]
                aug0 = (heads * pl.program_id(1) + head) * AUG_LANES
                qh = jnp.concatenate(
                    [jnp.where(_head_lanes(lane, hh), q2, jnp.zeros_like(q2)),
                     jnp.where((lane >= aug0) & (lane < aug0 + AUG_LANES), a2, jnp.zeros_like(a2))], axis=1)
                k_of = lambda a, b: kx_ref[pp, a:b, :]
            else:
                qh = q_ref[0, q0:q1, head * SLOT:(head + 1) * SLOT]
                k_of = lambda a, b: k_ref[0, a:b, head * SLOT:(head + 1) * SLOT]
            pieces = [(0, _dot_nt(qh, k_of(0, q0)))] if i else []
            return pieces + [(q0, jnp.where(diag_ok, _dot_nt(qh, k_of(q0, q1)), NEG_INF))]

        return (score_blocks, lambda a, b: v_ref[0, a:b, pp * LANES:(pp + 1) * LANES], slice(q0, q1), head)

    _run_units([unit(i, head) for i in _longest_first(seq // tq) for head in range(heads)], s_ref, o_ref)


def _prefix_attn(q, k, v, aq=None, ak=None, *, frame_causal, name, casts=()):
    batch, seq, _ = v.shape
    groups = HEADS // (2 * PAIRS_PER_STEP)
    decay = aq is not None
    vw = PAIRS_PER_STEP * LANES
    qk_w = vw if decay else 2 * PAIRS_PER_STEP * SLOT
    group_spec = lambda w: pl.BlockSpec((1, seq, w), lambda b, g: (b, 0, g))
    in_specs = [group_spec(qk_w), group_spec(qk_w), group_spec(vw)]
    args = [q, k, v]
    scratch = [pltpu.VMEM((PIPE_DEPTH, ATTN_TQ, seq), F32)]
    if decay:
        aug_spec = pl.BlockSpec((1, seq, LANES), lambda b, g: (b, 0, 0))
        in_specs += [aug_spec, aug_spec]
        args += [aq, ak]
        scratch += [pltpu.VMEM((PAIRS_PER_STEP, seq, 2 * LANES), BF16)]
    cast_in, cast_out, cast_shapes = _cast_specs(casts, batch * groups, lambda b, g: b * groups + g)
    return pl.pallas_call(
        functools.partial(_prefix_attn_kernel, frame_causal=frame_causal, decay=decay, n_cast=len(casts)),
        grid=(batch, groups),
        in_specs=in_specs + cast_in,
        out_specs=[group_spec(vw)] + cast_out,
        out_shape=[jax.ShapeDtypeStruct((batch, seq, BRANCH_WIDTH), BF16)] + cast_shapes,
        scratch_shapes=scratch,
        compiler_params=_params(2),
        name=name,
    )(*args, *[c.stacked for c in casts])


BAND_BLOCKS = C_LEFT_CHUNKS * CHUNK // ATTN_TQ + 1
BAND_WIDTH = BAND_BLOCKS * ATTN_TQ
BAND_PERIOD = BAND_WIDTH + ATTN_TQ


def _band_attn_kernel(q_ref, k_ref, v_ref, w_ref, *rest, n_cast):
    _convert_blocks(rest[:n_cast], rest[n_cast + 1:2 * n_cast + 1])
    o_ref, (bias_ref, s_ref) = rest[n_cast], rest[2 * n_cast + 1:]
    heads = 2 * PAIRS_PER_STEP
    seq = q_ref.shape[1]
    tq = ATTN_TQ
    lane = lax.broadcasted_iota(jnp.int32, (1, LANES), 1)

    @pl.when(pl.program_id(1) == 0)
    def _():
        r = lax.broadcasted_iota(jnp.int32, (tq, BAND_WIDTH), 0)
        c = lax.broadcasted_iota(jnp.int32, (tq, BAND_WIDTH), 1)
        dchunk = r // CHUNK - (c // CHUNK - C_LEFT_CHUNKS)
        valid = (dchunk >= 0) & (dchunk <= C_LEFT_CHUNKS)
        for head in range(heads):
            table = jnp.broadcast_to(w_ref[head], (tq, BAND_PERIOD))
            toeplitz = pltpu.roll(table, 0, 1, stride=1, stride_axis=0)
            bias_ref[head] = jnp.where(valid, toeplitz[:, :BAND_WIDTH], NEG_INF)

    def unit(t, head):
        k0, k1 = max(0, t - (BAND_BLOCKS - 1)) * tq, (t + 1) * tq
        boff = BAND_WIDTH - (k1 - k0)
        pp, hh = divmod(head, 2)
        cols = slice(pp * LANES, (pp + 1) * LANES)

        def score_blocks():
            q2 = q_ref[0, t * tq:(t + 1) * tq, cols]
            qh = jnp.where(_head_lanes(lane, hh), q2, jnp.zeros_like(q2))
            return [(0, _dot_nt(qh, k_ref[0, k0:k1, cols]) + bias_ref[head, :, boff:])]

        return (score_blocks, lambda a, b: v_ref[0, k0 + a:k0 + b, cols], slice(t * tq, (t + 1) * tq), head)

    _run_units([unit(t, head) for t in _longest_first(seq // tq) for head in range(heads)], s_ref, o_ref)


def _band_attn(q, k, v, w, casts=()):
    batch, seq, _ = v.shape
    heads = 2 * PAIRS_PER_STEP
    groups = HEADS // heads
    group_spec = pl.BlockSpec((1, seq, PAIRS_PER_STEP * LANES), lambda g, b: (b, 0, g))
    cast_in, cast_out, cast_shapes = _cast_specs(casts, batch * groups, lambda g, b: g * batch + b)
    return pl.pallas_call(
        functools.partial(_band_attn_kernel, n_cast=len(casts)),
        grid=(groups, batch),
        in_specs=[group_spec, group_spec, group_spec,
                  pl.BlockSpec((heads, 1, BAND_PERIOD), lambda g, b: (g, 0, 0))] + cast_in,
        out_specs=[group_spec] + cast_out,
        out_shape=[jax.ShapeDtypeStruct((batch, seq, BRANCH_WIDTH), BF16)] + cast_shapes,
        scratch_shapes=[pltpu.VMEM((heads, ATTN_TQ, BAND_WIDTH), F32),
                        pltpu.VMEM((PIPE_DEPTH, ATTN_TQ, BAND_WIDTH), F32)],
        compiler_params=pltpu.CompilerParams(dimension_semantics=("parallel", "arbitrary"),
                                             vmem_limit_bytes=VMEM_LIMIT),
        name="band_attn",
    )(q, k, v, w, *[c.stacked for c in casts])


def _merge_kernel(x_ref, ya_ref, yb_ref, yc_ref, gmix_ref, wg_ref, bg_ref, wb_ref, wo_ref, o_ref):
    x = x_ref[...]
    h = _row_rms(x, gmix_ref[...]).astype(BF16)
    merged = None
    for n, y_ref in enumerate((ya_ref, yb_ref, yc_ref)):
        cols = slice(n * D_MODEL, (n + 1) * D_MODEL)
        gate = jax.nn.sigmoid(_dot_nt(h, wg_ref[cols, :]) + bg_ref[:, cols])
        term = gate * _dot(y_ref[...], wb_ref[n])
        merged = term if merged is None else merged + term
    o_ref[...] = x + _dot(merged.astype(BF16), wo_ref[...])


def _merge(x, ya, yb, yc, gmix, wg, bg, wb, wo):
    t = x.shape[0]
    tm = TOKEN_TILE
    row = lambda width: pl.BlockSpec((tm, width), lambda i: (i, 0))
    consts = [gmix, wg, bg, wb, wo]
    return pl.pallas_call(
        _merge_kernel,
        grid=(t // tm,),
        in_specs=[row(D_MODEL)] + [row(BRANCH_WIDTH)] * 3 + [_const_spec(c) for c in consts],
        out_specs=row(D_MODEL),
        out_shape=jax.ShapeDtypeStruct((t, D_MODEL), F32),
        compiler_params=_params(1),
        name="merge",
    )(x, ya, yb, yc, *_arrays(consts))


FF_CHUNK = 1024


def _ffn_kernel(x_ref, g_ref, w1_ref, w2_ref, o_ref):
    x = x_ref[...]
    h = _row_rms(x, g_ref[...]).astype(BF16)
    acc = x
    for c in range(D_FF // FF_CHUNK):
        a = jnp.maximum(_dot(h, w1_ref[:, c * FF_CHUNK:(c + 1) * FF_CHUNK]), 0.0)
        acc = acc + _dot((a * a).astype(BF16), w2_ref[c * FF_CHUNK:(c + 1) * FF_CHUNK, :])
    o_ref[...] = acc


def _ffn(x, g, w1, w2):
    t = x.shape[0]
    tm = TOKEN_TILE
    row = pl.BlockSpec((tm, D_MODEL), lambda i: (i, 0))
    return pl.pallas_call(
        _ffn_kernel,
        grid=(t // tm,),
        in_specs=[row] + [_const_spec(c) for c in (g, w1, w2)],
        out_specs=row,
        out_shape=jax.ShapeDtypeStruct((t, D_MODEL), F32),
        compiler_params=_params(1),
        name="ffn",
    )(x, *_arrays((g, w1, w2)))


def _group_matrix(groups):
    ids = np.asarray(groups)
    return jnp.asarray((ids[:, None] == ids[None, :]) & (ids[:, None] >= 0), dtype=BF16)


def _band_table(rel_bias):
    rb = rel_bias.astype(F32) * LOG2E
    left = BAND_WIDTH - ATTN_TQ - REL_CLIP
    far = jnp.broadcast_to(rb[..., 2 * REL_CLIP:], rb.shape[:-1] + (BAND_PERIOD,))
    w = jnp.concatenate([far[..., :left], rb[..., ::-1], far[..., left + 2 * REL_CLIP + 1:]], axis=-1)
    return w[..., None, :]


def _rope_tables(seq):
    half = B_ROPE // 2
    inv = ROPE_THETA ** (-jnp.arange(half, dtype=F32) / half)
    ang = jnp.arange(seq).astype(F32)[:, None] * inv[None, :]
    cos, sin = jnp.cos(ang), jnp.sin(ang)
    pad_l = jnp.ones((seq, MISC_KR), F32)
    pad_r = jnp.ones((seq, SLOT - MISC_KR - B_ROPE), F32)
    cos_t = jnp.concatenate([pad_l, cos, cos, pad_r], axis=1)
    sin_t = jnp.concatenate([0 * pad_l, -sin, sin, 0 * pad_r], axis=1)
    return cos_t, sin_t


def kernel(x, norm_mix, w_in, b_forget, b_gate, qk_norm_a, mla_q_norm, mla_kv_norm, w_q_up, w_kv_up,
           qk_norm_b_nope, qk_norm_b_rope, qk_norm_c, rel_bias, w_branch, w_out, norm_ffn, w_ff1, w_ff2):
    batch, seq, _ = x.shape
    depth = w_in.shape[0]
    o = IN_OFFS
    zeros = lambda *s: jnp.zeros(s, F32)

    w_in_t = jnp.swapaxes(w_in, 1, 2)

    attn_scale = HEAD_DIM ** -0.5 * LOG2E
    tile8 = lambda g: jnp.tile(g, (1, HEADS))
    hgain = jnp.stack([tile8(qk_norm_a[:, 0]) * attn_scale, tile8(qk_norm_a[:, 1]),
                       tile8(qk_norm_c[:, 0]) * attn_scale, tile8(qk_norm_c[:, 1])], axis=1)

    mla_scale = (B_NOPE + B_ROPE) ** -0.5 * LOG2E
    pad = SLOT - B_NOPE - B_ROPE
    wq_slot = jnp.pad(w_q_up.reshape(depth, B_Q_LORA, HEADS, B_NOPE + B_ROPE),
                      ((0, 0), (0, 0), (0, 0), (0, pad))).reshape(depth, B_Q_LORA, HEADS * SLOT).astype(BF16)
    wkv = w_kv_up.reshape(depth, B_KV_LORA, HEADS, 2 * B_NOPE)
    wk_slot = jnp.pad(wkv[..., :B_NOPE], ((0, 0), (0, 0), (0, 0), (0, SLOT - B_NOPE))
                      ).reshape(depth, B_KV_LORA, HEADS * SLOT).astype(BF16)
    wv = wkv[..., B_NOPE:].reshape(depth, B_KV_LORA, BRANCH_WIDTH).astype(BF16)
    gq_slot = tile8(jnp.concatenate([qk_norm_b_nope[:, 0], qk_norm_b_rope[:, 0], zeros(depth, pad)], axis=1)
                    ) * mla_scale
    gk_slot = tile8(jnp.concatenate([qk_norm_b_nope[:, 1], zeros(depth, SLOT - B_NOPE)], axis=1))
    gkr_slot = jnp.concatenate([zeros(depth, MISC_KR), qk_norm_b_rope[:, 1], zeros(depth, pad)], axis=1)
    slot_groups = np.concatenate([np.zeros(B_NOPE), np.ones(B_ROPE), -np.ones(pad)]).astype(np.int64)
    slot_groups2 = np.concatenate([slot_groups, np.where(slot_groups >= 0, slot_groups + 2, -1)])
    gslot = _group_matrix(slot_groups2)
    invcnt = jnp.asarray(np.tile(np.concatenate([np.full(B_NOPE, 1.0 / B_NOPE), np.full(B_ROPE, 1.0 / B_ROPE),
                                                  np.ones(pad)]), 2)[None, :], dtype=F32)
    cos_t, sin_t = _rope_tables(seq)
    bf_rows = jnp.pad(b_forget.astype(F32), ((0, 0), (0, LANES - HEADS)))
    band_tables = _band_table(rel_bias)

    w_branch_rows = w_branch.reshape(depth, N_BRANCH * BRANCH_WIDTH, D_MODEL)

    xt = x.reshape(batch * seq, D_MODEL)
    seq3 = lambda a: a.reshape(batch, seq, a.shape[-1])
    for l in range(depth):
        qa, ka, va, qc, kc, vc, qdn, kvdn, misc = _in_proj(
            xt, norm_mix[l][None], _Layer(w_in_t, l, rows=o[10]), _Layer(hgain, l), mla_q_norm[l][None],
            mla_kv_norm[l][None])
        qb, kb, vb, aq, ak = _mla_prep(
            qdn, kvdn, misc, cos_t, sin_t, _Layer(wq_slot, l), _Layer(wk_slot, l), _Layer(wv, l), gslot, invcnt,
            gq_slot[l][None], gk_slot[l][None], gkr_slot[l][None], bf_rows[l][None], seq)
        ya, w_gate_b = _prefix_attn(seq3(qa), seq3(ka), seq3(va), seq3(aq), seq3(ak), frame_causal=True,
                                    name="fox_attn", casts=(_Cast(w_in_t, l, row0=o[10], rows=o[11] - o[10]),))
        yb, w_out_b, w_branch_b = _prefix_attn(seq3(qb), seq3(kb), seq3(vb), frame_causal=False, name="mla_attn",
                                               casts=(_Cast(w_out, l), _Cast(w_branch_rows, l)))
        yc, w_ff1_b, w_ff2_b = _band_attn(seq3(qc), seq3(kc), seq3(vc), band_tables[l],
                                          casts=(_Cast(w_ff1, l), _Cast(w_ff2, l)))
        flat = lambda a: a.reshape(batch * seq, BRANCH_WIDTH)
        xt = _merge(xt, flat(ya), flat(yb), flat(yc), norm_mix[l][None], w_gate_b, b_gate[l][None],
                    w_branch_b.reshape(N_BRANCH, BRANCH_WIDTH, D_MODEL), w_out_b)
        xt = _ffn(xt, norm_ffn[l][None], w_ff1_b, w_ff2_b)
    return xt.reshape(batch, seq, D_MODEL)
```

```python
import functools
from typing import NamedTuple

import numpy as np
import jax
import jax.numpy as jnp
from jax import lax
from jax.experimental import pallas as pl
from jax.experimental.pallas import tpu as pltpu

F32 = jnp.float32
BF16 = jnp.bfloat16

D_MODEL = 1024
DEPTH = 4
CHUNK = 64
EPS = 1e-6
NEG_INF = -1e30
HEADS = 8
HEAD_DIM = 64
B_NOPE = 64
B_ROPE = 32
B_Q_LORA = 384
B_KV_LORA = 256
ROPE_THETA = 10000.0
C_LEFT_CHUNKS = 8
REL_CLIP = 256
N_BRANCH = 3
BRANCH_WIDTH = 512
D_FF = 4 * D_MODEL
IN_SIZES = (512, 512, 512, HEADS, B_Q_LORA, B_KV_LORA, B_ROPE, 512, 512, 512, N_BRANCH * D_MODEL)
IN_OFFS = tuple(int(v) for v in np.cumsum((0,) + IN_SIZES))

LANES = 128
MXU_DIM = 256
SLOT = LANES
TOKEN_TILE = 1024
ATTN_TQ = 256
PAIRS_PER_STEP = 2
PIPE_DEPTH = 2
VMEM_LIMIT = 56 * 1024 * 1024

MISC_KR = 64
LOG2E = float(np.log2(np.e))


def _dot(a, b):
    return jnp.dot(a, b, preferred_element_type=F32)


def _dot_nt(a, b):
    return lax.dot_general(a, b, (((1,), (1,)), ((), ())), preferred_element_type=F32)


def _split2(v):
    hi = v.astype(BF16)
    lo = (v - hi.astype(F32)).astype(BF16)
    return hi, lo


def _group_sumsq(u, g_ref):
    hi, lo = _split2(u * u)
    g = g_ref[...]
    return _dot(hi, g) + _dot(lo, g)


def _row_rms(x, gain):
    ms = jnp.mean(x * x, axis=-1, keepdims=True)
    return x * lax.rsqrt(ms + EPS) * gain


class _Layer(NamedTuple):
    stacked: jax.Array
    index: int
    rows: int = 0


def _const_spec(op):
    if isinstance(op, _Layer):
        tail = op.stacked.shape[1:]
        if op.rows:
            tail = (op.rows,) + tail[1:]
        return pl.BlockSpec((None,) + tail, lambda *_: (op.index,) + (0,) * len(tail),
                            pipeline_mode=pl.Buffered(1))
    return pl.BlockSpec(op.shape, lambda *_: (0,) * op.ndim, pipeline_mode=pl.Buffered(1))


def _arrays(ops):
    return [op.stacked if isinstance(op, _Layer) else op for op in ops]


def _params(n_axes):
    return pltpu.CompilerParams(dimension_semantics=("parallel",) * n_axes,
                                vmem_limit_bytes=VMEM_LIMIT)


def _in_proj_kernel(x_ref, gmix_ref, wt_ref, hgain_ref, gq_ref, gkv_ref,
                    qa_ref, ka_ref, va_ref, qc_ref, kc_ref, vc_ref, qd_ref, kvd_ref, misc_ref):
    h = _row_rms(x_ref[...], gmix_ref[...]).astype(BF16)
    o = IN_OFFS

    first_head = lax.broadcasted_iota(jnp.int32, (1, LANES), 1) < HEAD_DIM

    def project(a, b):
        return _dot_nt(h, wt_ref[a:b, :].astype(BF16))

    def head_normed(a, gain_row):
        u = project(a, a + 512)
        parts = []
        for j in range(512 // LANES):
            ub = u[:, LANES * j:LANES * (j + 1)]
            sq = ub * ub
            s0 = jnp.sum(jnp.where(first_head, sq, 0.0), axis=-1, keepdims=True)
            s1 = jnp.sum(jnp.where(first_head, 0.0, sq), axis=-1, keepdims=True)
            parts.append(ub * lax.rsqrt(jnp.where(first_head, s0, s1) * (1.0 / HEAD_DIM) + EPS))
        return (jnp.concatenate(parts, axis=1) * hgain_ref[gain_row:gain_row + 1, :]).astype(BF16)

    qa_ref[...] = head_normed(o[0], 0)
    ka_ref[...] = head_normed(o[1], 1)
    va_ref[...] = project(o[2], o[3]).astype(BF16)
    qc_ref[...] = head_normed(o[7], 2)
    kc_ref[...] = head_normed(o[8], 3)
    vc_ref[...] = project(o[9], o[10]).astype(BF16)
    zrows = lambda n: jnp.zeros((n, D_MODEL), F32)
    w_tail = jnp.concatenate([wt_ref[o[4]:o[6], :], wt_ref[o[3]:o[4], :], zrows(MISC_KR - HEADS),
                              wt_ref[o[6]:o[7], :], zrows(LANES - MISC_KR - B_ROPE)], axis=0)
    tail = _dot_nt(h, w_tail.astype(BF16))
    qd_ref[...] = _row_rms(tail[:, :B_Q_LORA], gq_ref[...]).astype(BF16)
    kvd_ref[...] = _row_rms(tail[:, B_Q_LORA:B_Q_LORA + B_KV_LORA], gkv_ref[...]).astype(BF16)
    misc_ref[...] = tail[:, B_Q_LORA + B_KV_LORA:]


def _in_proj(x, gmix, w_t, hgain, gq, gkv):
    t = x.shape[0]
    tm = TOKEN_TILE
    row = lambda width: pl.BlockSpec((tm, width), lambda i: (i, 0))
    out_shapes = [jax.ShapeDtypeStruct((t, 512), BF16)] * 6 + [
        jax.ShapeDtypeStruct((t, B_Q_LORA), BF16),
        jax.ShapeDtypeStruct((t, B_KV_LORA), BF16),
        jax.ShapeDtypeStruct((t, LANES), F32)]
    consts = [gmix, w_t, hgain, gq, gkv]
    return pl.pallas_call(
        _in_proj_kernel,
        grid=(t // tm,),
        in_specs=[row(D_MODEL)] + [_const_spec(c) for c in consts],
        out_specs=[row(512)] * 6 + [row(B_Q_LORA), row(B_KV_LORA), row(LANES)],
        out_shape=out_shapes,
        compiler_params=_params(1),
        name="in_proj",
    )(x, *_arrays(consts))


def _split3(v):
    hi = v.astype(BF16)
    r1 = v - hi.astype(F32)
    mid = r1.astype(BF16)
    lo = (r1 - mid.astype(F32)).astype(BF16)
    return hi, mid, lo


AUG_LANES = LANES // HEADS


def _forget_gate_lanes(misc, gate_refs, aq_ref, ak_ref, carry_ref):
    bf_ref, tri_ref, pq_ref, pk_ref, oq_ref, ok_ref = gate_refs
    blk = MXU_DIM
    lane = lax.broadcasted_iota(jnp.int32, (blk, LANES), 1)
    tri = tri_ref[...]
    local = []
    for j in range(misc.shape[0] // blk):
        z = misc[j * blk:(j + 1) * blk, :] + bf_ref[...]
        lf = -(jnp.maximum(-z, 0.0) + jnp.log1p(jnp.exp(-jnp.abs(z))))
        hi, mid, lo = _split3(jnp.where(lane < HEADS, lf, 0.0))
        local.append(_dot(tri, hi) + _dot(tri, mid) + _dot(tri, lo))
    carry = carry_ref[...]
    for j, loc in enumerate(local):
        rows = slice(j * blk, (j + 1) * blk)
        c = loc + carry
        carry = c[blk - 1:blk, :]
        c3 = jnp.concatenate(_split3(c * LOG2E), axis=1)
        aq_ref[rows, :] = (_dot(c3, pq_ref[...]) + oq_ref[...]).astype(BF16)
        ak_ref[rows, :] = (_dot(c3, pk_ref[...]) + ok_ref[...]).astype(BF16)
    carry_ref[...] = carry


def _gate_constants(bf_row):
    pq = np.zeros((3 * LANES, LANES), np.float32)
    pk = np.zeros((3 * LANES, LANES), np.float32)
    oq = np.zeros((1, LANES), np.float32)
    ok = np.zeros((1, LANES), np.float32)
    for h in range(HEADS):
        for t in range(3):
            pq[t * LANES + h, h * AUG_LANES + t] = 1.0
            pk[t * LANES + h, h * AUG_LANES + 3 + t] = -1.0
            oq[0, h * AUG_LANES + 3 + t] = 1.0
            ok[0, h * AUG_LANES + t] = 1.0
    tri = jnp.asarray(np.tril(np.ones((MXU_DIM, MXU_DIM))), dtype=BF16)
    return [bf_row, tri, jnp.asarray(pq, BF16), jnp.asarray(pk, BF16), jnp.asarray(oq), jnp.asarray(ok)]


def _rope_slot(x, cos, sin, lo_half):
    partner = jnp.where(lo_half, pltpu.roll(x, LANES - B_ROPE // 2, 1), pltpu.roll(x, B_ROPE // 2, 1))
    return x * cos + partner * sin


def _mla_prep_kernel(qd_ref, kvd_ref, misc_ref, cos_ref, sin_ref, wq_ref, wk_ref, wv_ref,
                     gslot_ref, invcnt_ref, gq_ref, gk_ref, gkr_ref, *rest, tiles_per_seq):
    gate_refs, (q_ref, k_ref, v_ref, aq_ref, ak_ref, carry_ref) = rest[:6], rest[6:]

    @pl.when(pl.program_id(0) % tiles_per_seq == 0)
    def _():
        carry_ref[...] = jnp.zeros_like(carry_ref)

    _forget_gate_lanes(misc_ref[...], gate_refs, aq_ref, ak_ref, carry_ref)
    tm = qd_ref.shape[0]
    lane = lax.broadcasted_iota(jnp.int32, (tm, SLOT), 1)
    lo_half = lane < MISC_KR + B_ROPE // 2
    in_kr = (lane >= MISC_KR) & (lane < MISC_KR + B_ROPE)
    cos = cos_ref[...]
    sin = sin_ref[...]
    qd = qd_ref[...]
    kvd = kvd_ref[...]

    def slot_normed(u, gain_ref, col):
        ss = _group_sumsq(u, gslot_ref)
        return u * lax.rsqrt(ss * invcnt_ref[...] + EPS) * gain_ref[:, col:col + MXU_DIM]

    m = misc_ref[...]
    ss_kr = jnp.sum(jnp.where(in_kr, m * m, 0.0), axis=-1, keepdims=True)
    kr = _rope_slot(m * lax.rsqrt(ss_kr * (1.0 / B_ROPE) + EPS) * gkr_ref[...], cos, sin, lo_half)

    for j in range(HEADS * SLOT // MXU_DIM):
        col = j * MXU_DIM
        qn = slot_normed(_dot(qd, wq_ref[:, col:col + MXU_DIM]), gq_ref, col)
        kn = slot_normed(_dot(kvd, wk_ref[:, col:col + MXU_DIM]), gk_ref, col)
        for s in range(MXU_DIM // SLOT):
            sl = slice(s * SLOT, (s + 1) * SLOT)
            q_ref[:, col + s * SLOT:col + (s + 1) * SLOT] = _rope_slot(qn[:, sl], cos, sin, lo_half).astype(BF16)
            k_ref[:, col + s * SLOT:col + (s + 1) * SLOT] = (kn[:, sl] + kr).astype(BF16)
    v_ref[...] = _dot(kvd, wv_ref[...]).astype(BF16)


def _mla_prep(qdn, kvdn, misc, cos_t, sin_t, wq, wk, wv, gslot, invcnt, gq, gk, gkr, bf_row, seq):
    t = qdn.shape[0]
    tm = TOKEN_TILE
    per_seq = seq // tm
    row = lambda width: pl.BlockSpec((tm, width), lambda i: (i, 0))
    pos = pl.BlockSpec((tm, SLOT), lambda i: (i % per_seq, 0))
    consts = [wq, wk, wv, gslot, invcnt, gq, gk, gkr] + _gate_constants(bf_row)
    return pl.pallas_call(
        functools.partial(_mla_prep_kernel, tiles_per_seq=per_seq),
        grid=(t // tm,),
        in_specs=[row(B_Q_LORA), row(B_KV_LORA), row(LANES), pos, pos] + [_const_spec(c) for c in consts],
        out_specs=[row(HEADS * SLOT), row(HEADS * SLOT), row(BRANCH_WIDTH), row(LANES), row(LANES)],
        out_shape=[jax.ShapeDtypeStruct((t, HEADS * SLOT), BF16),
                   jax.ShapeDtypeStruct((t, HEADS * SLOT), BF16),
                   jax.ShapeDtypeStruct((t, BRANCH_WIDTH), BF16),
                   jax.ShapeDtypeStruct((t, LANES), BF16),
                   jax.ShapeDtypeStruct((t, LANES), BF16)],
        scratch_shapes=[pltpu.VMEM((1, LANES), F32)],
        compiler_params=pltpu.CompilerParams(dimension_semantics=("arbitrary",), vmem_limit_bytes=VMEM_LIMIT),
        name="mla_prep",
    )(qdn, kvdn, misc, cos_t, sin_t, *_arrays(consts))


def _head_lanes(lane, hh):
    return (lane >= hh * HEAD_DIM) & (lane < (hh + 1) * HEAD_DIM)


def _longest_first(n):
    return list(range(n - 1, -1, -1))


def _run_units(units, s_ref, o_ref):
    depth = s_ref.shape[0]
    lane = lax.broadcasted_iota(jnp.int32, (1, LANES), 1)
    stats = {}

    def scores(n):
        pieces = units[n][0]()
        for c0, blk in pieces:
            s_ref[n % depth, :, c0:c0 + blk.shape[1]] = blk
        stats[n] = (functools.reduce(jnp.maximum, [jnp.max(blk, axis=-1, keepdims=True) for _, blk in pieces]),
                    sum(blk.shape[1] for _, blk in pieces))

    def finish(n):
        _, values, rows, head = units[n]
        m, klen = stats.pop(n)
        p = jnp.exp2(s_ref[n % depth, :, 0:klen] - m)
        l = jnp.sum(p, axis=-1, keepdims=True)
        cols = slice(head // 2 * LANES, (head // 2 + 1) * LANES)
        o = (_dot(p.astype(BF16), values(0, klen))[:, cols] / l).astype(BF16)
        if head % 2 == 0:
            o_ref[0, rows, cols] = o
        else:
            o_ref[0, rows, cols] = jnp.where(lane < HEAD_DIM, o_ref[0, rows, cols], o)

    for n in range(min(depth, len(units))):
        scores(n)
    for n in range(len(units)):
        finish(n)
        if n + depth < len(units):
            scores(n + depth)


class _Cast(NamedTuple):
    stacked: jax.Array
    index: int
    row0: int = 0
    rows: int = 0


def _cast_specs(casts, steps, step_of):
    in_specs, out_specs, out_shapes = [], [], []
    for c in casts:
        _, rows, cols = c.stacked.shape
        rows = c.rows or rows
        blk = rows // steps
        if c.row0:
            in_specs.append(pl.BlockSpec((pl.Element(1), pl.Element(blk), pl.Element(cols)),
                                         lambda *ids, c=c, blk=blk: (
                                             c.index, pl.multiple_of(c.row0 + step_of(*ids) * blk, 8), 0)))
        else:
            in_specs.append(pl.BlockSpec((None, blk, cols), lambda *ids, c=c: (c.index, step_of(*ids), 0)))
        out_specs.append(pl.BlockSpec((blk, cols), lambda *ids: (step_of(*ids), 0)))
        out_shapes.append(jax.ShapeDtypeStruct((rows, cols), BF16))
    return in_specs, out_specs, out_shapes


def _convert_blocks(src_refs, dst_refs):
    for src, dst in zip(src_refs, dst_refs):
        dst[...] = src[...].reshape(dst.shape).astype(BF16)


def _prefix_attn_kernel(*refs, frame_causal, decay, n_cast):
    heads = 2 * PAIRS_PER_STEP
    n_in = 5 if decay else 3
    _convert_blocks(refs[n_in:n_in + n_cast], refs[n_in + n_cast + 1:n_in + 2 * n_cast + 1])
    refs = refs[:n_in] + (refs[n_in + n_cast],) + refs[n_in + 2 * n_cast + 1:]
    if decay:
        q_ref, k_ref, v_ref, aq_ref, ak_ref, o_ref, s_ref, kx_ref = refs
        for pp in range(PAIRS_PER_STEP):
            kx_ref[pp, :, 0:LANES] = k_ref[0, :, pp * LANES:(pp + 1) * LANES]
            kx_ref[pp, :, LANES:2 * LANES] = ak_ref[0]
    else:
        q_ref, k_ref, v_ref, o_ref, s_ref = refs
    seq = q_ref.shape[1]
    tq = ATTN_TQ
    lane = lax.broadcasted_iota(jnp.int32, (1, LANES), 1)
    row = lax.broadcasted_iota(jnp.int32, (tq, tq), 0)
    col = lax.broadcasted_iota(jnp.int32, (tq, tq), 1)
    if frame_causal:
        diag_ok = col <= row
    else:
        diag_ok = (col // CHUNK) <= (row // CHUNK)

    def unit(i, head):
        q0, q1 = i * tq, (i + 1) * tq
        pp, hh = divmod(head, 2)

        def score_blocks():
            if decay:
                q2 = q_ref[0, q0:q1, pp * LANES:(pp + 1) * LANES]
                a2 = aq_ref[0, q0:q1, :]
                aug0 = (heads * pl.program_id(1) + head) * AUG_LANES
                qh = jnp.concatenate(
                    [jnp.where(_head_lanes(lane, hh), q2, jnp.zeros_like(q2)),
                     jnp.where((lane >= aug0) & (lane < aug0 + AUG_LANES), a2, jnp.zeros_like(a2))], axis=1)
                k_of = lambda a, b: kx_ref[pp, a:b, :]
            else:
                qh = q_ref[0, q0:q1, head * SLOT:(head + 1) * SLOT]
                k_of = lambda a, b: k_ref[0, a:b, head * SLOT:(head + 1) * SLOT]
            pieces = [(0, _dot_nt(qh, k_of(0, q0)))] if i else []
            return pieces + [(q0, jnp.where(diag_ok, _dot_nt(qh, k_of(q0, q1)), NEG_INF))]

        return (score_blocks, lambda a, b: v_ref[0, a:b, :], slice(q0, q1), head)

    _run_units([unit(i, head) for i in _longest_first(seq // tq) for head in range(heads)], s_ref, o_ref)


def _prefix_attn(q, k, v, aq=None, ak=None, *, frame_causal, name, casts=()):
    batch, seq, _ = v.shape
    groups = HEADS // (2 * PAIRS_PER_STEP)
    decay = aq is not None
    vw = PAIRS_PER_STEP * LANES
    qk_w = vw if decay else 2 * PAIRS_PER_STEP * SLOT
    group_spec = lambda w: pl.BlockSpec((1, seq, w), lambda b, g: (b, 0, g))
    in_specs = [group_spec(qk_w), group_spec(qk_w), group_spec(vw)]
    args = [q, k, v]
    scratch = [pltpu.VMEM((PIPE_DEPTH, ATTN_TQ, seq), F32)]
    if decay:
        aug_spec = pl.BlockSpec((1, seq, LANES), lambda b, g: (b, 0, 0))
        in_specs += [aug_spec, aug_spec]
        args += [aq, ak]
        scratch += [pltpu.VMEM((PAIRS_PER_STEP, seq, 2 * LANES), BF16)]
    cast_in, cast_out, cast_shapes = _cast_specs(casts, batch * groups, lambda b, g: b * groups + g)
    return pl.pallas_call(
        functools.partial(_prefix_attn_kernel, frame_causal=frame_causal, decay=decay, n_cast=len(casts)),
        grid=(batch, groups),
        in_specs=in_specs + cast_in,
        out_specs=[group_spec(vw)] + cast_out,
        out_shape=[jax.ShapeDtypeStruct((batch, seq, BRANCH_WIDTH), BF16)] + cast_shapes,
        scratch_shapes=scratch,
        compiler_params=_params(2),
        name=name,
    )(*args, *[c.stacked for c in casts])


BAND_BLOCKS = C_LEFT_CHUNKS * CHUNK // ATTN_TQ + 1
BAND_WIDTH = BAND_BLOCKS * ATTN_TQ
BAND_PERIOD = BAND_WIDTH + ATTN_TQ


def _band_attn_kernel(q_ref, k_ref, v_ref, w_ref, *rest, n_cast):
    _convert_blocks(rest[:n_cast], rest[n_cast + 1:2 * n_cast + 1])
    o_ref, (bias_ref, s_ref) = rest[n_cast], rest[2 * n_cast + 1:]
    heads = 2 * PAIRS_PER_STEP
    seq = q_ref.shape[1]
    tq = ATTN_TQ
    lane = lax.broadcasted_iota(jnp.int32, (1, LANES), 1)

    @pl.when(pl.program_id(1) == 0)
    def _():
        r = lax.broadcasted_iota(jnp.int32, (tq, BAND_WIDTH), 0)
        c = lax.broadcasted_iota(jnp.int32, (tq, BAND_WIDTH), 1)
        dchunk = r // CHUNK - (c // CHUNK - C_LEFT_CHUNKS)
        valid = (dchunk >= 0) & (dchunk <= C_LEFT_CHUNKS)
        for head in range(heads):
            table = jnp.broadcast_to(w_ref[head], (tq, BAND_PERIOD))
            toeplitz = pltpu.roll(table, 0, 1, stride=1, stride_axis=0)
            bias_ref[head] = jnp.where(valid, toeplitz[:, :BAND_WIDTH], NEG_INF)

    def unit(t, head):
        k0, k1 = max(0, t - (BAND_BLOCKS - 1)) * tq, (t + 1) * tq
        boff = BAND_WIDTH - (k1 - k0)
        pp, hh = divmod(head, 2)
        cols = slice(pp * LANES, (pp + 1) * LANES)

        def score_blocks():
            q2 = q_ref[0, t * tq:(t + 1) * tq, cols]
            qh = jnp.where(_head_lanes(lane, hh), q2, jnp.zeros_like(q2))
            return [(0, _dot_nt(qh, k_ref[0, k0:k1, cols]) + bias_ref[head, :, boff:])]

        return (score_blocks, lambda a, b: v_ref[0, k0 + a:k0 + b, :], slice(t * tq, (t + 1) * tq), head)

    _run_units([unit(t, head) for t in _longest_first(seq // tq) for head in range(heads)], s_ref, o_ref)


def _band_attn(q, k, v, w, casts=()):
    batch, seq, _ = v.shape
    heads = 2 * PAIRS_PER_STEP
    groups = HEADS // heads
    group_spec = pl.BlockSpec((1, seq, PAIRS_PER_STEP * LANES), lambda g, b: (b, 0, g))
    cast_in, cast_out, cast_shapes = _cast_specs(casts, batch * groups, lambda g, b: g * batch + b)
    return pl.pallas_call(
        functools.partial(_band_attn_kernel, n_cast=len(casts)),
        grid=(groups, batch),
        in_specs=[group_spec, group_spec, group_spec,
                  pl.BlockSpec((heads, 1, BAND_PERIOD), lambda g, b: (g, 0, 0))] + cast_in,
        out_specs=[group_spec] + cast_out,
        out_shape=[jax.ShapeDtypeStruct((batch, seq, BRANCH_WIDTH), BF16)] + cast_shapes,
        scratch_shapes=[pltpu.VMEM((heads, ATTN_TQ, BAND_WIDTH), F32),
                        pltpu.VMEM((PIPE_DEPTH, ATTN_TQ, BAND_WIDTH), F32)],
        compiler_params=pltpu.CompilerParams(dimension_semantics=("parallel", "arbitrary"),
                                             vmem_limit_bytes=VMEM_LIMIT),
        name="band_attn",
    )(q, k, v, w, *[c.stacked for c in casts])


def _merge_kernel(x_ref, ya_ref, yb_ref, yc_ref, gmix_ref, wg_ref, bg_ref, wb_ref, wo_ref, o_ref):
    x = x_ref[...]
    h = _row_rms(x, gmix_ref[...]).astype(BF16)
    merged = None
    for n, y_ref in enumerate((ya_ref, yb_ref, yc_ref)):
        cols = slice(n * D_MODEL, (n + 1) * D_MODEL)
        gate = jax.nn.sigmoid(_dot_nt(h, wg_ref[cols, :]) + bg_ref[:, cols])
        term = gate * _dot(y_ref[...], wb_ref[n])
        merged = term if merged is None else merged + term
    o_ref[...] = x + _dot(merged.astype(BF16), wo_ref[...])


def _merge(x, ya, yb, yc, gmix, wg, bg, wb, wo):
    t = x.shape[0]
    tm = TOKEN_TILE
    row = lambda width: pl.BlockSpec((tm, width), lambda i: (i, 0))
    consts = [gmix, wg, bg, wb, wo]
    return pl.pallas_call(
        _merge_kernel,
        grid=(t // tm,),
        in_specs=[row(D_MODEL)] + [row(BRANCH_WIDTH)] * 3 + [_const_spec(c) for c in consts],
        out_specs=row(D_MODEL),
        out_shape=jax.ShapeDtypeStruct((t, D_MODEL), F32),
        compiler_params=_params(1),
        name="merge",
    )(x, ya, yb, yc, *_arrays(consts))


FF_CHUNK = 1024


def _ffn_kernel(x_ref, g_ref, w1_ref, w2_ref, o_ref):
    x = x_ref[...]
    h = _row_rms(x, g_ref[...]).astype(BF16)
    acc = x
    for c in range(D_FF // FF_CHUNK):
        a = jnp.maximum(_dot(h, w1_ref[:, c * FF_CHUNK:(c + 1) * FF_CHUNK]), 0.0)
        acc = acc + _dot((a * a).astype(BF16), w2_ref[c * FF_CHUNK:(c + 1) * FF_CHUNK, :])
    o_ref[...] = acc


def _ffn(x, g, w1, w2):
    t = x.shape[0]
    tm = TOKEN_TILE
    row = pl.BlockSpec((tm, D_MODEL), lambda i: (i, 0))
    return pl.pallas_call(
        _ffn_kernel,
        grid=(t // tm,),
        in_specs=[row] + [_const_spec(c) for c in (g, w1, w2)],
        out_specs=row,
        out_shape=jax.ShapeDtypeStruct((t, D_MODEL), F32),
        compiler_params=_params(1),
        name="ffn",
    )(x, *_arrays((g, w1, w2)))


def _group_matrix(groups):
    ids = np.asarray(groups)
    return jnp.asarray((ids[:, None] == ids[None, :]) & (ids[:, None] >= 0), dtype=BF16)


def _band_table(rel_bias):
    rb = rel_bias.astype(F32) * LOG2E
    left = BAND_WIDTH - ATTN_TQ - REL_CLIP
    far = jnp.broadcast_to(rb[..., 2 * REL_CLIP:], rb.shape[:-1] + (BAND_PERIOD,))
    w = jnp.concatenate([far[..., :left], rb[..., ::-1], far[..., left + 2 * REL_CLIP + 1:]], axis=-1)
    return w[..., None, :]


def _rope_tables(seq):
    half = B_ROPE // 2
    inv = ROPE_THETA ** (-jnp.arange(half, dtype=F32) / half)
    ang = jnp.arange(seq).astype(F32)[:, None] * inv[None, :]
    cos, sin = jnp.cos(ang), jnp.sin(ang)
    pad_l = jnp.ones((seq, MISC_KR), F32)
    pad_r = jnp.ones((seq, SLOT - MISC_KR - B_ROPE), F32)
    cos_t = jnp.concatenate([pad_l, cos, cos, pad_r], axis=1)
    sin_t = jnp.concatenate([0 * pad_l, -sin, sin, 0 * pad_r], axis=1)
    return cos_t, sin_t


def kernel(x, norm_mix, w_in, b_forget, b_gate, qk_norm_a, mla_q_norm, mla_kv_norm, w_q_up, w_kv_up,
           qk_norm_b_nope, qk_norm_b_rope, qk_norm_c, rel_bias, w_branch, w_out, norm_ffn, w_ff1, w_ff2):
    batch, seq, _ = x.shape
    depth = w_in.shape[0]
    o = IN_OFFS
    zeros = lambda *s: jnp.zeros(s, F32)

    w_in_t = jnp.swapaxes(w_in, 1, 2)

    attn_scale = HEAD_DIM ** -0.5 * LOG2E
    tile8 = lambda g: jnp.tile(g, (1, HEADS))
    hgain = jnp.stack([tile8(qk_norm_a[:, 0]) * attn_scale, tile8(qk_norm_a[:, 1]),
                       tile8(qk_norm_c[:, 0]) * attn_scale, tile8(qk_norm_c[:, 1])], axis=1)

    mla_scale = (B_NOPE + B_ROPE) ** -0.5 * LOG2E
    pad = SLOT - B_NOPE - B_ROPE
    wq_slot = jnp.pad(w_q_up.reshape(depth, B_Q_LORA, HEADS, B_NOPE + B_ROPE),
                      ((0, 0), (0, 0), (0, 0), (0, pad))).reshape(depth, B_Q_LORA, HEADS * SLOT).astype(BF16)
    wkv = w_kv_up.reshape(depth, B_KV_LORA, HEADS, 2 * B_NOPE)
    wk_slot = jnp.pad(wkv[..., :B_NOPE], ((0, 0), (0, 0), (0, 0), (0, SLOT - B_NOPE))
                      ).reshape(depth, B_KV_LORA, HEADS * SLOT).astype(BF16)
    wv = wkv[..., B_NOPE:].reshape(depth, B_KV_LORA, BRANCH_WIDTH).astype(BF16)
    gq_slot = tile8(jnp.concatenate([qk_norm_b_nope[:, 0], qk_norm_b_rope[:, 0], zeros(depth, pad)], axis=1)
                    ) * mla_scale
    gk_slot = tile8(jnp.concatenate([qk_norm_b_nope[:, 1], zeros(depth, SLOT - B_NOPE)], axis=1))
    gkr_slot = jnp.concatenate([zeros(depth, MISC_KR), qk_norm_b_rope[:, 1], zeros(depth, pad)], axis=1)
    slot_groups = np.concatenate([np.zeros(B_NOPE), np.ones(B_ROPE), -np.ones(pad)]).astype(np.int64)
    slot_groups2 = np.concatenate([slot_groups, np.where(slot_groups >= 0, slot_groups + 2, -1)])
    gslot = _group_matrix(slot_groups2)
    invcnt = jnp.asarray(np.tile(np.concatenate([np.full(B_NOPE, 1.0 / B_NOPE), np.full(B_ROPE, 1.0 / B_ROPE),
                                                  np.ones(pad)]), 2)[None, :], dtype=F32)
    cos_t, sin_t = _rope_tables(seq)
    bf_rows = jnp.pad(b_forget.astype(F32), ((0, 0), (0, LANES - HEADS)))
    band_tables = _band_table(rel_bias)

    w_branch_rows = w_branch.reshape(depth, N_BRANCH * BRANCH_WIDTH, D_MODEL)

    xt = x.reshape(batch * seq, D_MODEL)
    seq3 = lambda a: a.reshape(batch, seq, a.shape[-1])
    for l in range(depth):
        qa, ka, va, qc, kc, vc, qdn, kvdn, misc = _in_proj(
            xt, norm_mix[l][None], _Layer(w_in_t, l, rows=o[10]), _Layer(hgain, l), mla_q_norm[l][None],
            mla_kv_norm[l][None])
        qb, kb, vb, aq, ak = _mla_prep(
            qdn, kvdn, misc, cos_t, sin_t, _Layer(wq_slot, l), _Layer(wk_slot, l), _Layer(wv, l), gslot, invcnt,
            gq_slot[l][None], gk_slot[l][None], gkr_slot[l][None], bf_rows[l][None], seq)
        ya, w_gate_b = _prefix_attn(seq3(qa), seq3(ka), seq3(va), seq3(aq), seq3(ak), frame_causal=True,
                                    name="fox_attn", casts=(_Cast(w_in_t, l, row0=o[10], rows=o[11] - o[10]),))
        yb, w_out_b, w_branch_b = _prefix_attn(seq3(qb), seq3(kb), seq3(vb), frame_causal=False, name="mla_attn",
                                               casts=(_Cast(w_out, l), _Cast(w_branch_rows, l)))
        yc, w_ff1_b, w_ff2_b = _band_attn(seq3(qc), seq3(kc), seq3(vc), band_tables[l],
                                          casts=(_Cast(w_ff1, l), _Cast(w_ff2, l)))
        flat = lambda a: a.reshape(batch * seq, BRANCH_WIDTH)
        xt = _merge(xt, flat(ya), flat(yb), flat(yc), norm_mix[l][None], w_gate_b, b_gate[l][None],
                    w_branch_b.reshape(N_BRANCH, BRANCH_WIDTH, D_MODEL), w_out_b)
        xt = _ffn(xt, norm_ffn[l][None], w_ff1_b, w_ff2_b)
    return xt.reshape(batch, seq, D_MODEL)
```
